```python
import math
import jax, jax.numpy as jnp
from jax import lax
import numpy as np

D_MODEL = 1024
BATCH = 4
SEQ = 4096
DEPTH = 1
DEC_BATCH = 128
DEC_SEQ = 4
PAST_LEN = 8192
PAGE_SIZE = 128

H_A = 8
G_A = 2
R_A = H_A // G_A
HD_A = 64
W_A = H_A * HD_A
N_KV_A = 6
N_KV_SEL = 4
CMP_BLOCK = 32
CMP_STRIDE = 16
SEL_BLOCK = 64
N_SEL = 16
WINDOW = 512
FORCE_SCORE = 1e4
H_B = 4
NOPE = 128
ROPE_DIM = 64
V_HEAD = 128
Q_LORA = 256
KV_LORA = 128
W_B = H_B * V_HEAD
ROPE_BASE = 10000.0
N_BUCKETS = 32
MAX_DIST = 128
QBLK = 128
EPS = 1e-6
NEG_BIG = -1e30
IN_SIZES = (W_A, N_KV_A * G_A * HD_A, 3 * H_A, W_A, Q_LORA, KV_LORA, ROPE_DIM, W_B)
N_IN = W_A + N_KV_A * G_A * HD_A + 3 * H_A + W_A + Q_LORA + KV_LORA + ROPE_DIM + W_B

kernel_name = 'hymba_nsa_mla_decode_step'


def rmsnorm(x, g):
    xf = x.astype(jnp.float32)
    y = xf * lax.rsqrt(jnp.mean(xf * xf, axis=-1, keepdims=True) + EPS)
    return (y * g.astype(jnp.float32)).astype(x.dtype)


def masked_softmax(s, valid):
    s = jnp.where(valid, s.astype(jnp.float32), NEG_BIG)
    return jnp.where(valid, jax.nn.softmax(s, axis=-1), 0.0)


def rel_bucket(dist):
    n = jnp.maximum(dist, 0)
    max_exact = N_BUCKETS // 2
    nf = jnp.maximum(n, 1).astype(jnp.float32)
    large = max_exact + (jnp.log(nf / max_exact) / math.log(MAX_DIST / max_exact)
                         * (N_BUCKETS - max_exact)).astype(jnp.int32)
    large = jnp.minimum(large, N_BUCKETS - 1)
    return jnp.where(n < max_exact, n, large)


def apply_rope(x, pos):
    half = x.shape[-1] // 2
    inv = ROPE_BASE ** (-jnp.arange(half, dtype=jnp.float32) / half)
    ang = pos.astype(jnp.float32)[:, None] * inv[None, :]
    cos = jnp.cos(ang)[None, :, None, :]
    sin = jnp.sin(ang)[None, :, None, :]
    xf = x.astype(jnp.float32)
    x1, x2 = xf[..., :half], xf[..., half:]
    return jnp.concatenate([x1 * cos - x2 * sin, x1 * sin + x2 * cos], axis=-1).astype(x.dtype)


def project(x, c, q_pos, w_ada, b_ada, g_norm, w_in, g_q_lat, w_uq, g_kv_lat, w_uk):
    B, T = x.shape[0], x.shape[1]
    mod = c @ w_ada + b_ada
    shift, scale, gate = jnp.split(mod, 3, axis=-1)
    h = rmsnorm(x, g_norm) * (1.0 + scale[:, None, :]) + shift[:, None, :]
    u = h @ w_in
    cuts = []
    acc = 0
    for s in IN_SIZES[:-1]:
        acc += s
        cuts.append(acc)
    q_a, kv_a, g_logit, z_a, cq, ckv, kr, z_b = jnp.split(u, cuts, axis=-1)
    q_a = q_a.reshape(B, T, G_A, R_A, HD_A)
    kv_a = kv_a.reshape(B, T, N_KV_A, G_A, HD_A)
    gates = jax.nn.sigmoid(g_logit.reshape(B, T, G_A, R_A, 3))
    q_b = jnp.einsum('btq,qhe->bthe', rmsnorm(cq, g_q_lat), w_uq)
    q_rope = apply_rope(q_b[..., NOPE:], q_pos)
    q_lat = jnp.einsum('bthn,rhn->bthr', q_b[..., :NOPE], w_uk)
    k_rope = apply_rope(kr[:, :, None, :], q_pos)[:, :, 0, :]
    mla_row = jnp.concatenate([rmsnorm(ckv, g_kv_lat), k_rope], axis=-1)
    return gate, q_a, kv_a, gates, z_a, q_lat, q_rope, mla_row, z_b


def compress(k, w, pe):
    B, L = k.shape[0], k.shape[1]
    n_chunks = L // CMP_STRIDE
    ch = k[:, :n_chunks * CMP_STRIDE].reshape(B, n_chunks, CMP_STRIDE, G_A, HD_A)
    blk = jnp.concatenate([ch[:, :-1], ch[:, 1:]], axis=2) + pe[None, None, :, None, :]
    return jnp.einsum('bnlgd,lde->bnge', blk, w)


def cmp_block_ends(n_cmp):
    return jnp.arange(n_cmp, dtype=jnp.int32) * CMP_STRIDE + CMP_BLOCK - 1


def sel_blocks(k):
    B, L = k.shape[0], k.shape[1]
    pad = (-L) % SEL_BLOCK
    kp = jnp.pad(k, ((0, 0), (0, pad), (0, 0), (0, 0)))
    return kp.reshape(B, (L + pad) // SEL_BLOCK, SEL_BLOCK, G_A, HD_A)


def overlap_matrix(n_cmp, n_sel):
    cs = jnp.arange(n_cmp)[:, None] * CMP_STRIDE
    ss = jnp.arange(n_sel)[None, :] * SEL_BLOCK
    ov = jnp.minimum(cs + CMP_BLOCK, ss + SEL_BLOCK) - jnp.maximum(cs, ss)
    return jnp.maximum(ov, 0).astype(jnp.float32) / CMP_BLOCK


def nsa_cmp_sel(q, q_pos, kc, vc, c_end, ks_b, vs_b, m_ov, rel_bias):
    scale = HD_A ** -0.5
    table = rel_bias.reshape(N_BUCKETS, G_A, R_A)
    s = jnp.einsum('btgrd,bngd->btgrn', q, kc).astype(jnp.float32) * scale
    dist = q_pos[:, None] - c_end[None, :]
    bias = table[rel_bucket(dist)].transpose(0, 2, 3, 1)[None]
    p_c = masked_softmax(s + bias, (dist >= 0)[None, :, None, None, :])
    o_cmp = jnp.einsum('btgrn,bngd->btgrd', p_c.astype(vc.dtype), vc)
    imp = jnp.einsum('btgrn,ns->btgs', p_c, m_ov)
    ns = ks_b.shape[1]
    jj = jnp.arange(ns)[None, :]
    cc = (q_pos // SEL_BLOCK)[:, None]
    forced = (jj == 0) | (jj == cc) | (jj == cc - 1)
    score = jnp.where(forced[None, :, None, :], FORCE_SCORE,
                      jnp.where((jj <= cc)[None, :, None, :], imp, -jnp.inf))
    _, idx = lax.top_k(score, min(N_SEL, ns))
    idx_t = idx.transpose(0, 2, 1, 3)
    gather = jax.vmap(jax.vmap(lambda kb, ix: kb[ix]))
    gk = gather(ks_b.transpose(0, 3, 1, 2, 4), idx_t)
    gv = gather(vs_b.transpose(0, 3, 1, 2, 4), idx_t)
    tok = idx_t[..., None] * SEL_BLOCK + jnp.arange(SEL_BLOCK)
    dist_s = q_pos[None, None, :, None, None] - tok
    bias_s = table[rel_bucket(dist_s), jnp.arange(G_A)[None, :, None, None, None]]
    s2 = jnp.einsum('btgrd,bgtksd->bgtrks', q, gk).astype(jnp.float32) * scale
    s2 = s2 + bias_s.transpose(0, 1, 2, 5, 3, 4)
    B, G, T, R, K, SB = s2.shape
    valid = jnp.broadcast_to((dist_s >= 0)[:, :, :, None], s2.shape)
    p_s = masked_softmax(s2.reshape(B, G, T, R, K * SB), valid.reshape(B, G, T, R, K * SB))
    o_sel = jnp.einsum('bgtrks,bgtksd->btgrd', p_s.reshape(s2.shape).astype(gv.dtype), gv)
    return o_cmp, o_sel


def window_attend(q, q_pos, k, v, k_pos, rel_bias):
    scale = HD_A ** -0.5
    s = jnp.einsum('bntgrd,bnsgd->bngrts', q, k).astype(jnp.float32) * scale
    dist = q_pos[:, :, None] - k_pos[:, None, :]
    n, t, sk = dist.shape
    bias = rel_bias[rel_bucket(dist)].reshape(n, t, sk, G_A, R_A).transpose(0, 3, 4, 1, 2)
    valid = (dist >= 0) & (dist < WINDOW) & (k_pos[:, None, :] >= 0)
    p = masked_softmax(s + bias[None], valid[None, :, None, None])
    return jnp.einsum('bngrts,bnsgd->bntgrd', p.astype(v.dtype), v)


def mla_attend(q_lat, q_rope, ckv, k_rope, q_pos, k_pos):
    scale = (NOPE + ROPE_DIM) ** -0.5
    s = (jnp.einsum('bthr,blr->bhtl', q_lat, ckv)
         + jnp.einsum('bthe,ble->bhtl', q_rope, k_rope)).astype(jnp.float32) * scale
    valid = (k_pos[None, :] <= q_pos[:, None])[None, None]
    p = masked_softmax(s, valid)
    return jnp.einsum('bhtl,blr->bthr', p.astype(ckv.dtype), ckv)


def map_query_blocks(fn, qs, q_pos):
    T = q_pos.shape[0]
    nb = T // QBLK
    def to_blocks(a):
        return jnp.moveaxis(a.reshape(a.shape[0], nb, QBLK, *a.shape[2:]), 1, 0)
    def from_blocks(a):
        a = jnp.moveaxis(a, 0, 1)
        return a.reshape(a.shape[0], T, *a.shape[3:])
    outs = lax.map(lambda a: fn(*a[0], a[1]),
                   (tuple(to_blocks(q) for q in qs), q_pos.reshape(nb, QBLK)))
    return jax.tree_util.tree_map(from_blocks, outs)


def merge_out(x, gate, gates, o_cmp, o_sel, o_win, z_a, o_lat, z_b, w_uv, w_out):
    B, T = x.shape[0], x.shape[1]
    o_a = gates[..., 0:1] * o_cmp + gates[..., 1:2] * o_sel + gates[..., 2:3] * o_win
    y_a = o_a.reshape(B, T, W_A) * jax.nn.silu(z_a)
    y_b = jnp.einsum('bthr,rhv->bthv', o_lat, w_uv).reshape(B, T, W_B) * jax.nn.silu(z_b)
    out = jnp.concatenate([y_a, y_b], axis=-1) @ w_out
    return x + gate[:, None, :] * out


def prompt_layer(x, c, lw, cw, rel_bias, w_uv, w_out):
    B, T = x.shape[0], x.shape[1]
    pos = jnp.arange(T, dtype=jnp.int32)
    gate, q_a, kv_a, gates, z_a, q_lat, q_rope, mla_row, z_b = project(x, c, pos, *lw)
    w_ck, w_cv, pe_k, pe_v = cw
    kc = compress(kv_a[:, :, 0], w_ck, pe_k)
    vc = compress(kv_a[:, :, 1], w_cv, pe_v)
    c_end = cmp_block_ends(kc.shape[1])
    ks_b = sel_blocks(kv_a[:, :, 2])
    vs_b = sel_blocks(kv_a[:, :, 3])
    m_ov = overlap_matrix(kc.shape[1], ks_b.shape[1])
    o_cmp, o_sel = map_query_blocks(
        lambda q, p: nsa_cmp_sel(q, p, kc, vc, c_end, ks_b, vs_b, m_ov, rel_bias), (q_a,), pos)
    nb = T // QBLK
    nw = WINDOW // QBLK
    def band(a):
        ap = jnp.pad(a, ((0, 0), (WINDOW, 0), (0, 0), (0, 0))).reshape(B, nb + nw, QBLK, G_A, HD_A)
        return jnp.concatenate([ap[:, i:i + nb] for i in range(nw + 1)], axis=2)
    k_pos = jnp.arange(nb, dtype=jnp.int32)[:, None] * QBLK - WINDOW + jnp.arange((nw + 1) * QBLK, dtype=jnp.int32)[None, :]
    o_win = window_attend(q_a.reshape(B, nb, QBLK, G_A, R_A, HD_A), pos.reshape(nb, QBLK),
                          band(kv_a[:, :, 4]), band(kv_a[:, :, 5]), k_pos, rel_bias).reshape(B, T, G_A, R_A, HD_A)
    ckv, k_rope = mla_row[..., :KV_LORA], mla_row[..., KV_LORA:]
    o_lat = map_query_blocks(lambda ql, qr, p: mla_attend(ql, qr, ckv, k_rope, p, pos), (q_lat, q_rope), pos)
    x = merge_out(x, gate, gates, o_cmp, o_sel, o_win, z_a, o_lat, z_b, w_uv, w_out)
    new_win = kv_a[:, T - min(WINDOW, T):, 4:6]
    return x, kv_a[:, :, :N_KV_SEL], mla_row, new_win


def sample_layer(x, c, cache_nsa_l, cache_mla_l, win_l, page_table, lw, cw, rel_bias, w_uv, w_out):
    Bd, T = x.shape[0], x.shape[1]
    pos = PAST_LEN + jnp.arange(T, dtype=jnp.int32)
    gate, q_a, kv_a, gates, z_a, q_lat, q_rope, mla_row, z_b = project(x, c, pos, *lw)
    w_ck, w_cv, pe_k, pe_v = cw
    past_nsa = cache_nsa_l[page_table].reshape(Bd, PAST_LEN, N_KV_SEL, G_A, HD_A)
    full = jnp.concatenate([past_nsa, kv_a[:, :, :N_KV_SEL]], axis=1)
    kc = compress(full[:, :, 0], w_ck, pe_k)
    vc = compress(full[:, :, 1], w_cv, pe_v)
    c_end = cmp_block_ends(kc.shape[1])
    ks_b = sel_blocks(full[:, :, 2])
    vs_b = sel_blocks(full[:, :, 3])
    m_ov = overlap_matrix(kc.shape[1], ks_b.shape[1])
    o_cmp, o_sel = nsa_cmp_sel(q_a, pos, kc, vc, c_end, ks_b, vs_b, m_ov, rel_bias)
    wb = win_l.shape[1]
    kvw = jnp.concatenate([win_l, kv_a[:, :, 4:6]], axis=1)
    k_pos = (PAST_LEN - wb + jnp.arange(wb + T, dtype=jnp.int32))[None]
    o_win = window_attend(q_a[:, None], pos[None], kvw[:, None, :, 0], kvw[:, None, :, 1], k_pos, rel_bias)[:, 0]
    past_mla = cache_mla_l[page_table].reshape(Bd, PAST_LEN, KV_LORA + ROPE_DIM)
    full_mla = jnp.concatenate([past_mla, mla_row], axis=1)
    o_lat = mla_attend(q_lat, q_rope, full_mla[..., :KV_LORA], full_mla[..., KV_LORA:], pos,
                       jnp.arange(PAST_LEN + T, dtype=jnp.int32))
    x = merge_out(x, gate, gates, o_cmp, o_sel, o_win, z_a, o_lat, z_b, w_uv, w_out)
    return x, kv_a[:, :, :N_KV_SEL], mla_row, kvw[:, T:]


def setup_inputs(seed: int = 0) -> dict:
    key = jax.random.key(seed)
    ks = jax.random.split(key, 26)
    f32 = jnp.float32
    def nrm(k, shape, s):
        return jax.random.normal(k, shape, f32) * s
    n_pages = PAST_LEN // PAGE_SIZE
    n_phys = (DEC_BATCH * n_pages * 5) // 4
    wb = min(WINDOW, PAST_LEN)
    page_table = jax.random.permutation(ks[5], n_phys)[:DEC_BATCH * n_pages].reshape(DEC_BATCH, n_pages).astype(jnp.int32)
    return {
        'x_prompt': nrm(ks[0], (BATCH, SEQ, D_MODEL), 1.0),
        'x_sample': nrm(ks[1], (DEC_BATCH, DEC_SEQ, D_MODEL), 1.0),
        'cache_nsa': nrm(ks[2], (DEPTH, n_phys, PAGE_SIZE, N_KV_SEL, G_A, HD_A), 1.0),
        'cache_mla': nrm(ks[3], (DEPTH, n_phys, PAGE_SIZE, KV_LORA + ROPE_DIM), 1.0),
        'state_win': nrm(ks[4], (DEPTH, DEC_BATCH, wb, 2, G_A, HD_A), 1.0),
        'page_table': page_table,
        'c_prompt': nrm(ks[6], (BATCH, D_MODEL), 1.0),
        'c_sample': nrm(ks[7], (DEC_BATCH, D_MODEL), 1.0),
        'w_ada': nrm(ks[8], (DEPTH, D_MODEL, 3 * D_MODEL), 0.5 * D_MODEL ** -0.5),
        'b_ada': nrm(ks[9], (DEPTH, 3 * D_MODEL), 0.02),
        'g_norm': 1.0 + nrm(ks[10], (DEPTH, D_MODEL), 0.01),
        'w_in': nrm(ks[11], (DEPTH, D_MODEL, N_IN), D_MODEL ** -0.5),
        'w_cmp_k': nrm(ks[12], (DEPTH, CMP_BLOCK, HD_A, HD_A), (CMP_BLOCK * HD_A) ** -0.5),
        'w_cmp_v': nrm(ks[13], (DEPTH, CMP_BLOCK, HD_A, HD_A), (CMP_BLOCK * HD_A) ** -0.5),
        'pe_cmp_k': nrm(ks[14], (DEPTH, CMP_BLOCK, HD_A), 0.1),
        'pe_cmp_v': nrm(ks[15], (DEPTH, CMP_BLOCK, HD_A), 0.1),
        'rel_bias': nrm(ks[16], (N_BUCKETS, H_A), 0.1),
        'g_q_lat': 1.0 + nrm(ks[17], (DEPTH, Q_LORA), 0.01),
        'w_uq': nrm(ks[18], (DEPTH, Q_LORA, H_B, NOPE + ROPE_DIM), Q_LORA ** -0.5),
        'g_kv_lat': 1.0 + nrm(ks[19], (DEPTH, KV_LORA), 0.01),
        'w_uk': nrm(ks[20], (DEPTH, KV_LORA, H_B, NOPE), KV_LORA ** -0.5),
        'w_uv': nrm(ks[21], (DEPTH, KV_LORA, H_B, V_HEAD), KV_LORA ** -0.5),
        'w_out': nrm(ks[22], (DEPTH, W_A + W_B, D_MODEL), (W_A + W_B) ** -0.5),
        'g_final': 1.0 + nrm(ks[23], (D_MODEL,), 0.01),
    }


def reference(x_prompt, x_sample, cache_nsa, cache_mla, state_win, page_table, c_prompt, c_sample,
              w_ada, b_ada, g_norm, w_in, w_cmp_k, w_cmp_v, pe_cmp_k, pe_cmp_v, rel_bias,
              g_q_lat, w_uq, g_kv_lat, w_uk, w_uv, w_out, g_final):
    xp, xs = x_prompt, x_sample
    nsa_p, nsa_s, mla_p, mla_s, win_p, win_s = [], [], [], [], [], []
    for l in range(DEPTH):
        lw = (w_ada[l], b_ada[l], g_norm[l], w_in[l], g_q_lat[l], w_uq[l], g_kv_lat[l], w_uk[l])
        cw = (w_cmp_k[l], w_cmp_v[l], pe_cmp_k[l], pe_cmp_v[l])
        xp, rows_n, rows_m, rows_w = prompt_layer(xp, c_prompt, lw, cw, rel_bias, w_uv[l], w_out[l])
        nsa_p.append(rows_n)
        mla_p.append(rows_m)
        win_p.append(rows_w)
        xs, rows_n, rows_m, rows_w = sample_layer(xs, c_sample, cache_nsa[l], cache_mla[l], state_win[l],
                                                  page_table, lw, cw, rel_bias, w_uv[l], w_out[l])
        nsa_s.append(rows_n)
        mla_s.append(rows_m)
        win_s.append(rows_w)
    y_prompt = rmsnorm(xp, g_final)
    y_sample = rmsnorm(xs, g_final)
    return (y_prompt, y_sample, jnp.stack(nsa_p), jnp.stack(nsa_s), jnp.stack(mla_p), jnp.stack(mla_s),
            jnp.stack(win_p), jnp.stack(win_s))
```

```python
import functools
import math

import jax
import jax.numpy as jnp
from jax import lax
from jax.experimental import pallas as pl
from jax.experimental.pallas import tpu as pltpu

F32 = jnp.float32
BF16 = jnp.bfloat16

H_A, G_A, R_A, HD_A = 8, 2, 4, 64
W_A = H_A * HD_A
CMP_BLOCK, CMP_STRIDE = 32, 16
SEL_BLOCK, N_SEL = 64, 16
WINDOW = 512
FORCE_SCORE = 1e4
H_B, NOPE, ROPE_DIM, V_HEAD, Q_LORA, KV_LORA = 4, 128, 64, 128, 256, 128
W_B = H_B * V_HEAD
ROPE_BASE = 10000.0
N_BUCKETS, MAX_DIST = 32, 128
EPS = 1e-6
NEG = -1e30

TQ = 128
KT = 512
TM = 256
LANE = 128
PAGES_PER_STEP = 16
VMEM_LIMIT = 56 * 1024 * 1024


def _dot(a, b):
    return jnp.dot(a, b, preferred_element_type=F32)


def _dot_nt(a, b):
    return lax.dot_general(a, b, (((1,), (1,)), ((), ())), preferred_element_type=F32)


def _rel_bucket(dist):
    n = jnp.maximum(dist, 0)
    max_exact = N_BUCKETS // 2
    nf = jnp.maximum(n, 1).astype(F32)
    large = max_exact + (jnp.log(nf / max_exact) / math.log(MAX_DIST / max_exact)
                         * (N_BUCKETS - max_exact)).astype(jnp.int32)
    large = jnp.minimum(large, N_BUCKETS - 1)
    return jnp.where(n < max_exact, n, large)


def _bias_lookup(bucket, tab_ref, h):
    acc = jnp.zeros(bucket.shape, F32)
    for k in range(N_BUCKETS):
        acc = jnp.where(bucket == k, tab_ref[k, h], acc)
    return acc


def _softmax_rows(s):
    m = jnp.max(s, axis=-1, keepdims=True)
    p = jnp.exp(s - m)
    l = jnp.sum(p, axis=-1, keepdims=True)
    return jnp.where(s > 0.5 * NEG, p / l, 0.0)


def _mod_kernel(c_ref, w_ref, b_ref, o_ref):
    o_ref[...] = _dot(c_ref[...].astype(BF16), w_ref[...]) + b_ref[...]


def _modulation(c, w_ada_b, b_ada):
    n, d = c.shape
    n3 = w_ada_b.shape[1]
    tn = 1024
    return pl.pallas_call(
        _mod_kernel,
        grid=(n3 // tn,),
        in_specs=[pl.BlockSpec((n, d), lambda j: (0, 0)),
                  pl.BlockSpec((d, tn), lambda j: (0, j)),
                  pl.BlockSpec((1, tn), lambda j: (0, j))],
        out_specs=pl.BlockSpec((n, tn), lambda j: (0, j)),
        out_shape=jax.ShapeDtypeStruct((n, n3), F32),
        name="adaln_mod",
    )(c, w_ada_b, b_ada)


def _proj_kernel(x_ref, shift_ref, scale_ref, gn_ref, wt_ref, wf_ref, gq_ref, wqn_ref, wqr_ref,
                 wqs_ref, wuk_ref, gkv_ref, cos_ref, sin_ref, cost_ref, sint_ref,
                 qa_ref, gates_ref, sza_ref, szb_ref, qm_ref, nsat_ref, wint_ref, kvb_ref,
                 mlat_ref, mlab_ref, *, mla_scale):
    x = x_ref[...]
    y = x * lax.rsqrt(jnp.mean(x * x, axis=-1, keepdims=True) + EPS) * gn_ref[...]
    h = y * (1.0 + scale_ref[0]) + shift_ref[0]
    hb = h.astype(BF16)

    ut = _dot(hb, wt_ref[...])
    qa_ref[...] = ut[:, 0:W_A].astype(BF16)
    za = ut[:, W_A:2 * W_A]
    zb = ut[:, 2 * W_A:3 * W_A]
    sza_ref[...] = za * jax.nn.sigmoid(za)
    szb_ref[...] = zb * jax.nn.sigmoid(zb)
    cq = ut[:, 3 * W_A:3 * W_A + Q_LORA]
    gates_ref[...] = jax.nn.sigmoid(ut[:, 3 * W_A + Q_LORA:3 * W_A + Q_LORA + LANE])

    cqn = (cq * lax.rsqrt(jnp.mean(cq * cq, axis=-1, keepdims=True) + EPS) * gq_ref[...]).astype(BF16)
    qn = _dot(cqn, wqn_ref[...])
    qr = _dot(cqn, wqr_ref[...])
    qs = _dot(cqn, wqs_ref[...])
    q_rope = qr * cos_ref[...] + qs * sin_ref[...]
    for hh in range(H_B):
        q_lat = _dot(qn[:, hh * NOPE:(hh + 1) * NOPE].astype(BF16), wuk_ref[hh])
        qm_ref[:, 2 * LANE * hh:2 * LANE * hh + LANE] = (q_lat * mla_scale).astype(BF16)
        qm_ref[:, 2 * LANE * hh + LANE:2 * LANE * (hh + 1)] = (
            q_rope[:, hh * LANE:(hh + 1) * LANE] * mla_scale).astype(BF16)

    uf = _dot_nt(wf_ref[...], hb)
    n_sel_rows = 4 * G_A * HD_A
    n_kv_rows = 6 * G_A * HD_A
    nsat_ref[0] = uf[0:n_sel_rows]
    wint_ref[0] = uf[n_sel_rows:n_kv_rows]
    kvb_ref[0] = uf[0:n_kv_rows].astype(BF16)
    c = uf[n_kv_rows:n_kv_rows + KV_LORA]
    cn = c * lax.rsqrt(jnp.mean(c * c, axis=0, keepdims=True) + EPS) * gkv_ref[...]
    kr = uf[n_kv_rows + KV_LORA:n_kv_rows + KV_LORA + ROPE_DIM]
    half = ROPE_DIM // 2
    x1, x2 = kr[0:half], kr[half:ROPE_DIM]
    ct, st = cost_ref[...], sint_ref[...]
    mla = jnp.concatenate([cn, x1 * ct - x2 * st, x1 * st + x2 * ct], axis=0)
    mlat_ref[0] = mla
    mlab_ref[0] = jnp.concatenate([mla, jnp.zeros((2 * LANE - KV_LORA - ROPE_DIM, mla.shape[1]), F32)],
                                  axis=0).astype(BF16)


def _project(xf, shift3, scale3, per_row_mod, nb, tb, wts, cos_tm, sin_tm, cos_t, sin_t):
    n, d = xf.shape
    tiles_b = tb // TM
    ntab = cos_tm.shape[0] // TM
    (gn, wt, wf, gq, wqn, wqr, wqs, wuk, gkv) = wts
    if per_row_mod:
        mod_spec = pl.BlockSpec((1, TM, d), lambda r: (r, 0, 0))
    else:
        mod_spec = pl.BlockSpec((1, 1, d), lambda r: (r // tiles_b, 0, 0))

    def full(a):
        return pl.BlockSpec(a.shape, lambda r, _n=a.ndim: (0,) * _n)

    def fm(rows):
        return pl.BlockSpec((1, rows, TM), lambda r: (r // tiles_b, 0, r % tiles_b))

    def tmaj(cols):
        return pl.BlockSpec((TM, cols), lambda r: (r, 0))

    n_sel_rows = 4 * G_A * HD_A
    out_shape = (
        jax.ShapeDtypeStruct((n, W_A), BF16),
        jax.ShapeDtypeStruct((n, LANE), F32),
        jax.ShapeDtypeStruct((n, W_A), F32),
        jax.ShapeDtypeStruct((n, W_B), F32),
        jax.ShapeDtypeStruct((n, H_B * 2 * LANE), BF16),
        jax.ShapeDtypeStruct((nb, n_sel_rows, tb), F32),
        jax.ShapeDtypeStruct((nb, 2 * G_A * HD_A, tb), F32),
        jax.ShapeDtypeStruct((nb, 6 * G_A * HD_A, tb), BF16),
        jax.ShapeDtypeStruct((nb, KV_LORA + ROPE_DIM, tb), F32),
        jax.ShapeDtypeStruct((nb, 2 * LANE, tb), BF16),
    )
    out_specs = (tmaj(W_A), tmaj(LANE), tmaj(W_A), tmaj(W_B), tmaj(H_B * 2 * LANE),
                 fm(n_sel_rows), fm(2 * G_A * HD_A), fm(6 * G_A * HD_A), fm(KV_LORA + ROPE_DIM), fm(2 * LANE))
    in_specs = [tmaj(d), mod_spec, mod_spec, full(gn), full(wt), full(wf), full(gq), full(wqn),
                full(wqr), full(wqs), full(wuk), full(gkv),
                pl.BlockSpec((TM, H_B * LANE), lambda r: (r % ntab, 0)),
                pl.BlockSpec((TM, H_B * LANE), lambda r: (r % ntab, 0)),
                pl.BlockSpec((ROPE_DIM // 2, TM), lambda r: (0, r % ntab)),
                pl.BlockSpec((ROPE_DIM // 2, TM), lambda r: (0, r % ntab))]
    return pl.pallas_call(
        functools.partial(_proj_kernel, mla_scale=(NOPE + ROPE_DIM) ** -0.5),
        grid=(n // TM,),
        in_specs=in_specs,
        out_specs=out_specs,
        out_shape=out_shape,
        compiler_params=pltpu.CompilerParams(vmem_limit_bytes=VMEM_LIMIT),
        name="in_proj",
    )(xf, shift3, scale3, gn, wt, wf, gq, wqn, wqr, wqs, wuk, gkv, cos_tm, sin_tm, cos_t, sin_t)


def _chunk_rows(lperm_ref, feats_tok):
    return _dot_nt(lperm_ref[...], feats_tok).astype(BF16)


def _compress_chunks(a_ref, wc_ref, pe_ref, wflat_ref, nch):
    outs = []
    for kind in range(2):
        acc = jnp.zeros((nch, 2 * LANE), F32)
        for p in range(CMP_STRIDE // 2):
            lhs = jnp.concatenate([a_ref[2 * p, :, kind * LANE:(kind + 1) * LANE],
                                   a_ref[2 * p + 1, :, kind * LANE:(kind + 1) * LANE]], axis=1)
            acc = acc + _dot(lhs, wc_ref[kind, p])
        first, second = acc[:, 0:LANE], acc[:, LANE:2 * LANE]
        cb = _dot(pe_ref[kind], wflat_ref[kind])[0:1]
        outs.append(first + pltpu.roll(second, nch - 1, 0) + cb)
    return jnp.concatenate(outs, axis=1)


def _prompt_compress_kernel(kv_ref, lperm_ref, wc_ref, pe_ref, wflat_ref, o_ref, a_ref):
    t = kv_ref.shape[2]
    nch = t // CMP_STRIDE
    for tix in range(t // 256):
        tb = _chunk_rows(lperm_ref, kv_ref[0, :, tix * 256:(tix + 1) * 256])
        for j in range(CMP_STRIDE):
            a_ref[j, tix * 16:(tix + 1) * 16, :] = tb[16 * j:16 * (j + 1)]
    o_ref[0] = _compress_chunks(a_ref, wc_ref, pe_ref, wflat_ref, nch).astype(BF16)


def _prompt_compress(kvb, cw):
    nb, _, t = kvb.shape
    nch = t // CMP_STRIDE
    lperm, wc, pe, wflat = cw

    def full(a):
        return pl.BlockSpec(a.shape, lambda b, _n=a.ndim: (0,) * _n)

    return pl.pallas_call(
        _prompt_compress_kernel,
        grid=(nb,),
        in_specs=[pl.BlockSpec((1, 2 * LANE, t), lambda b: (b, 0, 0)), full(lperm), full(wc), full(pe), full(wflat)],
        out_specs=pl.BlockSpec((1, nch, 2 * LANE), lambda b: (b, 0, 0)),
        out_shape=jax.ShapeDtypeStruct((nb, nch, 2 * LANE), BF16),
        scratch_shapes=[pltpu.VMEM((CMP_STRIDE, nch, 2 * LANE), BF16)],
        compiler_params=pltpu.CompilerParams(vmem_limit_bytes=VMEM_LIMIT),
        name="prompt_compress",
    )(kvb, lperm, wc, pe, wflat)


def _rank_rows(score, n_valid):
    jj = lax.broadcasted_iota(jnp.int32, score.shape, 0)
    rank = jnp.zeros(score.shape, jnp.int32)
    for j2 in range(n_valid):
        row = score[j2:j2 + 1, :]
        before = (row > score) | ((row == score) & (jj > j2))
        rank = rank + before.astype(jnp.int32)
    return rank


def _rank_lanes(score, n_valid):
    jj = lax.broadcasted_iota(jnp.int32, score.shape, 1)
    rank = jnp.zeros(score.shape, jnp.int32)
    for j2 in range(n_valid):
        col = score[:, j2:j2 + 1]
        before = (col > score) | ((col == score) & (jj > j2))
        rank = rank + before.astype(jnp.int32)
    return rank


def _prompt_nsa_kernel(tab_ref, qa_ref, sel_ref, win_ref, cmp_ref, gates_ref, movt_ref, e_ref,
                       o_ref, bwin_ref, dsel_ref, dcmp_ref, sc_ref):
    i = pl.program_id(0)
    b = pl.program_id(1)
    t = sel_ref.shape[2]
    nch = cmp_ref.shape[1]
    nc = nch - 1
    ns = movt_ref.shape[0]
    n_win_tiles = WINDOW // TQ + 1

    @pl.when((i == 0) & (b == 0))
    def _():
        tl = lax.broadcasted_iota(jnp.int32, (TQ, n_win_tiles * TQ), 0)
        kl = lax.broadcasted_iota(jnp.int32, (TQ, n_win_tiles * TQ), 1)
        dist = tl + WINDOW - kl
        bucket = _rel_bucket(dist)
        ok = (dist >= 0) & (dist < WINDOW)
        tl2 = lax.broadcasted_iota(jnp.int32, (TQ, 2 * TQ), 0)
        kl2 = lax.broadcasted_iota(jnp.int32, (TQ, 2 * TQ), 1)
        dist2 = tl2 + TQ - kl2
        bucket2 = _rel_bucket(dist2)
        for h in range(H_A):
            bwin_ref[h] = jnp.where(ok, _bias_lookup(bucket, tab_ref, h), NEG)
            dsel_ref[h] = jnp.where(dist2 >= 0,
                                    _bias_lookup(bucket2, tab_ref, h) - tab_ref[N_BUCKETS - 1, h], NEG)

    @pl.when(b == 0)
    def _():
        tl = lax.broadcasted_iota(jnp.int32, (TQ, nch), 0)
        n = lax.broadcasted_iota(jnp.int32, (TQ, nch), 1)
        dist = i * TQ + tl - (n * CMP_STRIDE + CMP_BLOCK - 1)
        bucket = _rel_bucket(dist)
        ok = (dist >= 0) & (n < nc)
        for h in range(H_A):
            dcmp_ref[h] = jnp.where(ok, _bias_lookup(bucket, tab_ref, h) - tab_ref[N_BUCKETS - 1, h], NEG)

    n_kt = (i * TQ + TQ + KT - 1) // KT
    gates = gates_ref[...]
    for g in range(G_A):
        q2 = jnp.concatenate([qa_ref[:, HD_A * (R_A * g + r):HD_A * (R_A * g + r + 1)] for r in range(R_A)],
                             axis=0)
        kc = cmp_ref[0, :, HD_A * g:HD_A * (g + 1)]
        vc = cmp_ref[0, :, LANE + HD_A * g:LANE + HD_A * (g + 1)]
        s = _dot_nt(q2, kc).reshape(R_A, TQ, nch) + dcmp_ref[R_A * g:R_A * (g + 1)]
        p_c = _softmax_rows(s.reshape(R_A * TQ, nch))
        p_cb = p_c.astype(BF16)
        o_cmp = _dot(p_cb, vc)
        imp_t = jnp.zeros((ns, TQ), F32)
        for r in range(R_A):
            imp_t = imp_t + _dot_nt(movt_ref[...], p_cb[r * TQ:(r + 1) * TQ])
        jj = lax.broadcasted_iota(jnp.int32, (ns, TQ), 0)
        cc = (i * TQ + lax.broadcasted_iota(jnp.int32, (ns, TQ), 1)) // SEL_BLOCK
        forced = (jj == 0) | (jj == cc) | (jj == cc - 1)
        score = jnp.where(forced, FORCE_SCORE, jnp.where(jj <= cc, imp_t, -jnp.inf))
        rank = _rank_rows(score, ns)
        sel_t = ((rank < min(N_SEL, ns)) & (jj <= cc)).astype(F32)
        sel = jnp.transpose(sel_t).astype(BF16)

        def score_tile(kt, carry):
            c0 = pl.multiple_of(kt * KT, KT)
            ks = sel_ref[0, HD_A * g:HD_A * (g + 1), pl.ds(c0, KT)]
            madd = (_dot(sel, e_ref[:, pl.ds(c0, KT)]) - 1.0) * (-NEG)
            sc_ref[:, pl.ds(c0, KT)] = (_dot(q2, ks).reshape(R_A, TQ, KT) + madd[None]).reshape(R_A * TQ, KT)
            return carry

        lax.fori_loop(0, n_kt, score_tile, 0)
        d_here = dsel_ref[R_A * g:R_A * (g + 1)].reshape(R_A * TQ, 2 * TQ)
        c_diag = pl.multiple_of(i * TQ, TQ)
        sc_ref[:, pl.ds(c_diag, TQ)] += d_here[:, TQ:2 * TQ]

        @pl.when(i > 0)
        def _():
            c_prev = pl.multiple_of((i - 1) * TQ, TQ)
            sc_ref[:, pl.ds(c_prev, TQ)] += d_here[:, 0:TQ]

        def max_tile(kt, m):
            c0 = pl.multiple_of(kt * KT, KT)
            return jnp.maximum(m, jnp.max(sc_ref[:, pl.ds(c0, KT)], axis=-1, keepdims=True))

        m = lax.fori_loop(0, n_kt, max_tile, jnp.full((R_A * TQ, 1), NEG, F32))

        def pv_tile(kt, carry):
            l, acc = carry
            c0 = pl.multiple_of(kt * KT, KT)
            p = jnp.exp(sc_ref[:, pl.ds(c0, KT)] - m)
            vs = sel_ref[0, 2 * LANE // 2 + HD_A * g:2 * LANE // 2 + HD_A * (g + 1), pl.ds(c0, KT)]
            return l + jnp.sum(p, axis=-1, keepdims=True), acc + _dot_nt(p.astype(BF16), vs)

        l, acc = lax.fori_loop(0, n_kt, pv_tile,
                               (jnp.zeros((R_A * TQ, 1), F32), jnp.zeros((R_A * TQ, HD_A), F32)))
        o_sel = acc / l

        s_w = []
        for w in range(n_win_tiles):
            kt = i - (n_win_tiles - 1) + w
            c0 = pl.multiple_of(jnp.maximum(kt, 0) * TQ, TQ)
            kw = win_ref[0, HD_A * g:HD_A * (g + 1), pl.ds(c0, TQ)]
            bw = bwin_ref[R_A * g:R_A * (g + 1), :, w * TQ:(w + 1) * TQ].reshape(R_A * TQ, TQ)
            s_w.append(_dot(q2, kw) + bw + jnp.where(kt < 0, NEG, 0.0))
        p_w = _softmax_rows(jnp.concatenate(s_w, axis=1)).astype(BF16)
        o_win = jnp.zeros((R_A * TQ, HD_A), F32)
        for w in range(n_win_tiles):
            kt = i - (n_win_tiles - 1) + w
            c0 = pl.multiple_of(jnp.maximum(kt, 0) * TQ, TQ)
            vw = win_ref[0, LANE + HD_A * g:LANE + HD_A * (g + 1), pl.ds(c0, TQ)]
            o_win = o_win + _dot_nt(p_w[:, w * TQ:(w + 1) * TQ], vw)

        for r in range(R_A):
            h = R_A * g + r
            rows = slice(r * TQ, (r + 1) * TQ)
            o_ref[:, HD_A * h:HD_A * (h + 1)] = (gates[:, 3 * h:3 * h + 1] * o_cmp[rows]
                                                  + gates[:, 3 * h + 1:3 * h + 2] * o_sel[rows]
                                                  + gates[:, 3 * h + 2:3 * h + 3] * o_win[rows])


def _prompt_nsa(rel_bias, qa, kvb, cmpk, gates, movt, emat):
    nb, _, t = kvb.shape
    ni = t // TQ
    nch = cmpk.shape[1]
    n_win_tiles = WINDOW // TQ + 1

    def full(a):
        return pl.BlockSpec(a.shape, lambda i, b, _n=a.ndim: (0,) * _n)

    return pl.pallas_call(
        _prompt_nsa_kernel,
        grid=(ni, nb),
        in_specs=[pl.BlockSpec(memory_space=pltpu.SMEM),
                  pl.BlockSpec((TQ, W_A), lambda i, b: (b * ni + i, 0)),
                  pl.BlockSpec((1, 2 * LANE, t), lambda i, b: (b, 1, 0)),
                  pl.BlockSpec((1, 2 * LANE, t), lambda i, b: (b, 2, 0)),
                  pl.BlockSpec((1, nch, 2 * LANE), lambda i, b: (b, 0, 0)),
                  pl.BlockSpec((TQ, LANE), lambda i, b: (b * ni + i, 0)),
                  full(movt), full(emat)],
        out_specs=pl.BlockSpec((TQ, W_A), lambda i, b: (b * ni + i, 0)),
        out_shape=jax.ShapeDtypeStruct((nb * t, W_A), F32),
        scratch_shapes=[pltpu.VMEM((H_A, TQ, n_win_tiles * TQ), F32),
                        pltpu.VMEM((H_A, TQ, 2 * TQ), F32),
                        pltpu.VMEM((H_A, TQ, nch), F32),
                        pltpu.VMEM((R_A * TQ, t), F32)],
        compiler_params=pltpu.CompilerParams(vmem_limit_bytes=VMEM_LIMIT,
                                             dimension_semantics=("arbitrary", "arbitrary")),
        name="prompt_nsa",
    )(rel_bias, qa, kvb, kvb, cmpk, gates, movt, emat)


def _prompt_mla_kernel(qm_ref, mla_ref, o_ref, sc_ref):
    i = pl.program_id(1)
    n_kt = (i * TQ + TQ + KT - 1) // KT
    q2 = jnp.concatenate([qm_ref[:, 2 * LANE * h:2 * LANE * (h + 1)] for h in range(H_B)], axis=0)
    qpos = i * TQ + lax.broadcasted_iota(jnp.int32, (H_B, TQ, KT), 1).reshape(H_B * TQ, KT)
    klane = lax.broadcasted_iota(jnp.int32, (H_B * TQ, KT), 1)

    def score_tile(kt, m):
        c0 = pl.multiple_of(kt * KT, KT)
        s = _dot(q2, mla_ref[0, :, pl.ds(c0, KT)])
        s = jnp.where(kt * KT + klane <= qpos, s, NEG)
        sc_ref[:, pl.ds(c0, KT)] = s
        return jnp.maximum(m, jnp.max(s, axis=-1, keepdims=True))

    m = lax.fori_loop(0, n_kt, score_tile, jnp.full((H_B * TQ, 1), NEG, F32))

    def pv_tile(kt, carry):
        l, acc = carry
        c0 = pl.multiple_of(kt * KT, KT)
        p = jnp.exp(sc_ref[:, pl.ds(c0, KT)] - m)
        return (l + jnp.sum(p, axis=-1, keepdims=True),
                acc + _dot_nt(p.astype(BF16), mla_ref[0, 0:KV_LORA, pl.ds(c0, KT)]))

    l, acc = lax.fori_loop(0, n_kt, pv_tile,
                           (jnp.zeros((H_B * TQ, 1), F32), jnp.zeros((H_B * TQ, KV_LORA), F32)))
    o = acc / l
    for h in range(H_B):
        o_ref[:, KV_LORA * h:KV_LORA * (h + 1)] = o[h * TQ:(h + 1) * TQ]


def _prompt_mla(qm, mlab):
    nb, rows, t = mlab.shape
    ni = t // TQ
    return pl.pallas_call(
        _prompt_mla_kernel,
        grid=(nb, ni),
        in_specs=[pl.BlockSpec((TQ, H_B * 2 * LANE), lambda b, i: (b * ni + i, 0)),
                  pl.BlockSpec((1, rows, t), lambda b, i: (b, 0, 0))],
        out_specs=pl.BlockSpec((TQ, H_B * KV_LORA), lambda b, i: (b * ni + i, 0)),
        out_shape=jax.ShapeDtypeStruct((nb * t, H_B * KV_LORA), F32),
        scratch_shapes=[pltpu.VMEM((H_B * TQ, t), F32)],
        compiler_params=pltpu.CompilerParams(vmem_limit_bytes=VMEM_LIMIT),
        name="prompt_mla",
    )(qm, mlab)


def _merge_kernel(x_ref, gate_ref, oa_ref, ol_ref, sza_ref, szb_ref, wuv_ref, wout_ref, gf_ref, y_ref):
    ya = oa_ref[...] * sza_ref[...]
    ol = ol_ref[...]
    szb = szb_ref[...]
    yb = [_dot(ol[:, KV_LORA * h:KV_LORA * (h + 1)].astype(BF16), wuv_ref[h]) * szb[:, V_HEAD * h:V_HEAD * (h + 1)]
          for h in range(H_B)]
    yab = jnp.concatenate([ya] + yb, axis=1).astype(BF16)
    xn = x_ref[...] + gate_ref[0] * _dot(yab, wout_ref[...])
    y_ref[...] = xn * lax.rsqrt(jnp.mean(xn * xn, axis=-1, keepdims=True) + EPS) * gf_ref[...]


def _merge(xf, gate3, per_row_mod, tb, oa, ol, sza, szb, wuv, wout, gfin):
    n, d = xf.shape
    tiles_b = tb // TM
    if per_row_mod:
        mod_spec = pl.BlockSpec((1, TM, d), lambda r: (r, 0, 0))
    else:
        mod_spec = pl.BlockSpec((1, 1, d), lambda r: (r // tiles_b, 0, 0))

    def full(a):
        return pl.BlockSpec(a.shape, lambda r, _n=a.ndim: (0,) * _n)

    def tmaj(cols):
        return pl.BlockSpec((TM, cols), lambda r: (r, 0))

    return pl.pallas_call(
        _merge_kernel,
        grid=(n // TM,),
        in_specs=[tmaj(d), mod_spec, tmaj(W_A), tmaj(W_B), tmaj(W_A), tmaj(W_B), full(wuv), full(wout), full(gfin)],
        out_specs=tmaj(d),
        out_shape=jax.ShapeDtypeStruct((n, d), F32),
        compiler_params=pltpu.CompilerParams(vmem_limit_bytes=VMEM_LIMIT),
        name="out_merge",
    )(xf, gate3, oa, ol, sza, szb, wuv, wout, gfin)


def _row_head_bias(bucket, tab_ref, g, shift):
    ts = bucket.shape[0] // R_A
    row_r = lax.broadcasted_iota(jnp.int32, bucket.shape, 0) // ts
    out = jnp.zeros(bucket.shape, F32)
    for r in range(R_A):
        h = R_A * g + r
        val = _bias_lookup(bucket, tab_ref, h)
        if shift:
            val = val - tab_ref[N_BUCKETS - 1, h]
        out = jnp.where(row_r == r, val, out)
    return out


def _decode_nsa_kernel(pt_ref, tab_ref, *refs, ts, past):
    pages = refs[:PAGES_PER_STEP]
    (qa_ref, new_ref, win_ref, gsel_ref, lperm_ref, wc_ref, pe_ref, wflat_ref, mov_ref, e_ref,
     o_ref, wout_ref, a_ref, stash_ref, bc_ref, ds_ref, dn_ref, bw_ref, bwn_ref) = refs[PAGES_PER_STEP:]
    del pt_ref
    b = pl.program_id(0)
    s = pl.program_id(1)
    n_steps = pl.num_programs(1)
    nch = past // CMP_STRIDE
    nc = nch - 1
    nrow = R_A * ts
    n_cmp_rows = 2 * G_A * HD_A
    wb = win_ref.shape[2]

    for pp in range(PAGES_PER_STEP // 2):
        pa = pages[2 * pp][0]
        pb = pages[2 * pp + 1][0]
        pair = s * (PAGES_PER_STEP // 2) + pp
        both = jnp.concatenate([pa[0:n_cmp_rows], pb[0:n_cmp_rows]], axis=1).astype(BF16)
        tb = _chunk_rows(lperm_ref, both)
        c0 = pl.multiple_of(pair * 16, 16)
        for j in range(CMP_STRIDE):
            a_ref[j, pl.ds(c0, 16), :] = tb[16 * j:16 * (j + 1)]
        k0 = pl.multiple_of(pair * 256, 256)
        stash_ref[:, pl.ds(k0, 256)] = jnp.concatenate(
            [pa[n_cmp_rows:2 * n_cmp_rows], pb[n_cmp_rows:2 * n_cmp_rows]], axis=1).astype(BF16)

    @pl.when(s == n_steps - 1)
    def _():
        trow = lax.broadcasted_iota(jnp.int32, (nrow, 1), 0) % ts

        @pl.when(b == 0)
        def _():
            n = lax.broadcasted_iota(jnp.int32, (nrow, nch), 1)
            dist = past + trow - (n * CMP_STRIDE + CMP_BLOCK - 1)
            u = lax.broadcasted_iota(jnp.int32, (nrow, LANE), 1)
            dist_s = LANE + trow - u
            dist_n = trow - u
            kl = lax.broadcasted_iota(jnp.int32, (nrow, wb), 1)
            dist_w = wb + trow - kl
            ok_n = (dist_n >= 0) & (u < ts)
            for g in range(G_A):
                bc_ref[g] = jnp.where((dist >= 0) & (n < nc), _row_head_bias(_rel_bucket(dist), tab_ref, g, False), NEG)
                ds_ref[g] = _row_head_bias(_rel_bucket(dist_s), tab_ref, g, True)
                dn_ref[g] = jnp.where(ok_n, _row_head_bias(_rel_bucket(dist_n), tab_ref, g, True), NEG)
                bw_ref[g] = jnp.where(dist_w < WINDOW, _row_head_bias(_rel_bucket(dist_w), tab_ref, g, False), NEG)
                bwn_ref[g] = jnp.where(ok_n, _row_head_bias(_rel_bucket(dist_n), tab_ref, g, False), NEG)

        cmpk = _compress_chunks(a_ref, wc_ref, pe_ref, wflat_ref, nch).astype(BF16)
        new = new_ref[0]
        nsp = mov_ref.shape[1]
        nbp = past // SEL_BLOCK
        jj = lax.broadcasted_iota(jnp.int32, (nrow, nsp), 1)
        cc = (past + trow) // SEL_BLOCK
        for g in range(G_A):
            q2 = qa_ref[0, g]
            s_c = _dot_nt(q2, cmpk[:, HD_A * g:HD_A * (g + 1)]) + bc_ref[g]
            p_c = _softmax_rows(s_c).astype(BF16)
            o_cmp = _dot(p_c, cmpk[:, LANE + HD_A * g:LANE + HD_A * (g + 1)])
            imp = _dot(p_c, mov_ref[...])
            tot = imp
            for r in range(1, R_A):
                tot = tot + pltpu.roll(imp, r * ts, 0)
            forced = (jj == 0) | (jj == cc) | (jj == cc - 1)
            score = jnp.where(forced, FORCE_SCORE, jnp.where(jj <= cc, tot, -jnp.inf))
            rank = _rank_lanes(score, nbp + 1)
            sel = ((rank < N_SEL) & (jj <= cc)).astype(BF16)
            madd = (_dot(sel[:, 0:nbp], e_ref[...]) - 1.0) * (-NEG)
            s_p = _dot(q2, stash_ref[HD_A * g:HD_A * (g + 1), :]) + madd
            s_last = s_p[:, past - LANE:past] + ds_ref[g]
            k_new = new[2 * LANE + HD_A * g:2 * LANE + HD_A * (g + 1)].astype(BF16)
            v_new = new[3 * LANE + HD_A * g:3 * LANE + HD_A * (g + 1)].astype(BF16)
            s_n = _dot(q2, k_new) + dn_ref[g]
            s_all = jnp.concatenate([s_p[:, 0:past - LANE], s_last, s_n], axis=1)
            m = jnp.max(s_all, axis=-1, keepdims=True)
            p = jnp.exp(s_all - m)
            l = jnp.sum(p, axis=-1, keepdims=True)
            pb16 = p.astype(BF16)
            o_sel = (_dot_nt(pb16[:, 0:past], stash_ref[LANE + HD_A * g:LANE + HD_A * (g + 1), :])
                     + _dot_nt(pb16[:, past:past + LANE], v_new)) / l
            kw = win_ref[0, HD_A * g:HD_A * (g + 1), :].astype(BF16)
            vw = win_ref[0, LANE + HD_A * g:LANE + HD_A * (g + 1), :].astype(BF16)
            kw_new = new[4 * LANE + HD_A * g:4 * LANE + HD_A * (g + 1)].astype(BF16)
            vw_new = new[5 * LANE + HD_A * g:5 * LANE + HD_A * (g + 1)].astype(BF16)
            s_w = jnp.concatenate([_dot(q2, kw) + bw_ref[g], _dot(q2, kw_new) + bwn_ref[g]], axis=1)
            p_w = _softmax_rows(s_w).astype(BF16)
            o_win = _dot_nt(p_w[:, 0:wb], vw) + _dot_nt(p_w[:, wb:wb + LANE], vw_new)
            gs = gsel_ref[0, g]
            o_ref[0, g] = gs[:, 0:1] * o_cmp + gs[:, 1:2] * o_sel + gs[:, 2:3] * o_win

        win = win_ref[0]
        rolled = pltpu.roll(win, wb - ts, 1)
        new_w = pltpu.roll(new[4 * LANE:6 * LANE], LANE - ts, 1)
        lane = lax.broadcasted_iota(jnp.int32, (win.shape[0], LANE), 1)
        last = jnp.where(lane >= LANE - ts, new_w, rolled[:, wb - LANE:wb])
        wout_ref[0] = jnp.concatenate([rolled[:, 0:wb - LANE], last], axis=1)


def _decode_nsa(page_table, rel_bias, cache_t, qa_s, new_kv, win_t, gsel, cw, mov, emat, ts, past):
    bd, n_pages = page_table.shape
    n_steps = n_pages // PAGES_PER_STEP
    lperm, wc, pe, wflat = cw
    nrow = R_A * ts
    nch = past // CMP_STRIDE
    wb = win_t.shape[2]
    feat = cache_t.shape[1]

    def full(a):
        return pl.BlockSpec(a.shape, lambda b, s, pt, _n=a.ndim: (0,) * _n)

    def page_spec(k):
        return pl.BlockSpec((1, feat, LANE), lambda b, s, pt, _k=k: (pt[b, s * PAGES_PER_STEP + _k], 0, 0))

    in_specs = ([pl.BlockSpec(memory_space=pltpu.SMEM)]
                + [page_spec(k) for k in range(PAGES_PER_STEP)]
                + [pl.BlockSpec((1, G_A, nrow, HD_A), lambda b, s, pt: (b, 0, 0, 0)),
                   pl.BlockSpec((1, 6 * LANE, LANE), lambda b, s, pt: (b, 0, 0)),
                   pl.BlockSpec((1, 2 * LANE, wb), lambda b, s, pt: (b, 0, 0)),
                   pl.BlockSpec((1, G_A, nrow, LANE), lambda b, s, pt: (b, 0, 0, 0)),
                   full(lperm), full(wc), full(pe), full(wflat), full(mov), full(emat)])
    grid_spec = pltpu.PrefetchScalarGridSpec(
        num_scalar_prefetch=1,
        grid=(bd, n_steps),
        in_specs=in_specs,
        out_specs=[pl.BlockSpec((1, G_A, nrow, HD_A), lambda b, s, pt: (b, 0, 0, 0)),
                   pl.BlockSpec((1, 2 * LANE, wb), lambda b, s, pt: (b, 0, 0))],
        scratch_shapes=[pltpu.VMEM((CMP_STRIDE, nch, 2 * LANE), BF16),
                        pltpu.VMEM((2 * LANE, past), BF16),
                        pltpu.VMEM((G_A, nrow, nch), F32),
                        pltpu.VMEM((G_A, nrow, LANE), F32),
                        pltpu.VMEM((G_A, nrow, LANE), F32),
                        pltpu.VMEM((G_A, nrow, wb), F32),
                        pltpu.VMEM((G_A, nrow, LANE), F32)])
    return pl.pallas_call(
        functools.partial(_decode_nsa_kernel, ts=ts, past=past),
        grid_spec=grid_spec,
        out_shape=(jax.ShapeDtypeStruct((bd, G_A, nrow, HD_A), F32),
                   jax.ShapeDtypeStruct((bd, 2 * LANE, wb), F32)),
        compiler_params=pltpu.CompilerParams(vmem_limit_bytes=VMEM_LIMIT,
                                             dimension_semantics=("arbitrary", "arbitrary")),
        name="decode_nsa",
    )(page_table, rel_bias, *([cache_t] * PAGES_PER_STEP), qa_s, new_kv, win_t, gsel,
      lperm, wc, pe, wflat, mov, emat)


def _decode_mla_kernel(pt_ref, *refs, ts):
    pages = refs[:PAGES_PER_STEP]
    qm_ref, new_ref, o_ref, m_ref, l_ref, acc_ref = refs[PAGES_PER_STEP:]
    del pt_ref
    s = pl.program_id(1)
    n_steps = pl.num_programs(1)
    nrow = H_B * ts
    q2 = qm_ref[0]
    feat = pages[0].shape[1]
    zpad = jnp.zeros((2 * LANE - feat, LANE), BF16)

    @pl.when(s == 0)
    def _():
        m_ref[...] = jnp.full(m_ref.shape, NEG, F32)
        l_ref[...] = jnp.zeros(l_ref.shape, F32)
        acc_ref[...] = jnp.zeros(acc_ref.shape, F32)

    def update(kt, vt, sadd):
        sc = _dot(q2, kt)
        if sadd is not None:
            sc = sc + sadd
        m_old = m_ref[...]
        m_new = jnp.maximum(m_old, jnp.max(sc, axis=-1, keepdims=True))
        alpha = jnp.exp(m_old - m_new)
        p = jnp.exp(sc - m_new)
        l_ref[...] = alpha * l_ref[...] + jnp.sum(p, axis=-1, keepdims=True)
        acc_ref[...] = alpha * acc_ref[...] + _dot_nt(p.astype(BF16), vt)
        m_ref[...] = m_new

    pb = [jnp.concatenate([pg[0].astype(BF16), zpad], axis=0) for pg in pages]
    kt = jnp.concatenate(pb, axis=1)
    update(kt, kt[0:KV_LORA], None)

    @pl.when(s == n_steps - 1)
    def _():
        kn = jnp.concatenate([new_ref[0].astype(BF16), zpad], axis=0)
        trow = lax.broadcasted_iota(jnp.int32, (nrow, LANE), 0) % ts
        u = lax.broadcasted_iota(jnp.int32, (nrow, LANE), 1)
        update(kn, kn[0:KV_LORA], jnp.where((u <= trow) & (u < ts), 0.0, NEG))
        o_ref[0] = acc_ref[...] / l_ref[...]


def _decode_mla(page_table, cache_t, qm_s, new_mla, ts):
    bd, n_pages = page_table.shape
    n_steps = n_pages // PAGES_PER_STEP
    nrow = H_B * ts
    feat = cache_t.shape[1]

    def page_spec(k):
        return pl.BlockSpec((1, feat, LANE), lambda b, s, pt, _k=k: (pt[b, s * PAGES_PER_STEP + _k], 0, 0))

    grid_spec = pltpu.PrefetchScalarGridSpec(
        num_scalar_prefetch=1,
        grid=(bd, n_steps),
        in_specs=([page_spec(k) for k in range(PAGES_PER_STEP)]
                  + [pl.BlockSpec((1, nrow, 2 * LANE), lambda b, s, pt: (b, 0, 0)),
                     pl.BlockSpec((1, feat, LANE), lambda b, s, pt: (b, 0, 0))]),
        out_specs=pl.BlockSpec((1, nrow, KV_LORA), lambda b, s, pt: (b, 0, 0)),
        scratch_shapes=[pltpu.VMEM((nrow, 1), F32), pltpu.VMEM((nrow, 1), F32), pltpu.VMEM((nrow, KV_LORA), F32)])
    return pl.pallas_call(
        functools.partial(_decode_mla_kernel, ts=ts),
        grid_spec=grid_spec,
        out_shape=jax.ShapeDtypeStruct((bd, nrow, KV_LORA), F32),
        compiler_params=pltpu.CompilerParams(vmem_limit_bytes=VMEM_LIMIT,
                                             dimension_semantics=("arbitrary", "arbitrary")),
        name="decode_mla",
    )(page_table, *([cache_t] * PAGES_PER_STEP), qm_s, new_mla)


def _rope_tables(pos):
    half = ROPE_DIM // 2
    inv = ROPE_BASE ** (-jnp.arange(half, dtype=F32) / half)
    ang = pos.astype(F32)[:, None] * inv[None, :]
    cos, sin = jnp.cos(ang), jnp.sin(ang)
    z = jnp.zeros((pos.shape[0], LANE - ROPE_DIM), F32)
    cos_tm = jnp.tile(jnp.concatenate([cos, cos, z], axis=1), (1, H_B))
    sin_tm = jnp.tile(jnp.concatenate([sin, sin, z], axis=1), (1, H_B))
    return cos_tm, sin_tm, cos.T, sin.T


def _overlap_t(n_cmp_rows, n_sel):
    cs = jnp.arange(n_cmp_rows)[None, :] * CMP_STRIDE
    ss = jnp.arange(n_sel)[:, None] * SEL_BLOCK
    ov = jnp.minimum(cs + CMP_BLOCK, ss + SEL_BLOCK) - jnp.maximum(cs, ss)
    ov = jnp.maximum(ov, 0).astype(F32) / CMP_BLOCK
    return jnp.where(jnp.arange(n_cmp_rows)[None, :] < n_cmp_rows - 1, ov, 0.0)


def _compress_weights(w_k, w_v, pe_k, pe_v):
    eye = jnp.eye(G_A, dtype=F32)

    def pair_blocks(w):
        w6 = w.reshape(2, CMP_STRIDE // 2, 2, HD_A, HD_A)
        blk = jnp.einsum('hpjde,gk->pjgdhke', w6, eye)
        return blk.reshape(CMP_STRIDE // 2, 2 * LANE, 2 * LANE)

    wc = jnp.stack([pair_blocks(w_k), pair_blocks(w_v)]).astype(BF16)
    pe = jnp.stack([jnp.tile(pe_k.reshape(1, -1), (8, 1)), jnp.tile(pe_v.reshape(1, -1), (8, 1))]).astype(BF16)
    wflat = jnp.stack([jnp.tile(w_k.reshape(-1, HD_A), (1, G_A)),
                       jnp.tile(w_v.reshape(-1, HD_A), (1, G_A))]).astype(BF16)
    r = jnp.arange(256)
    lperm = (jnp.arange(256)[None, :] == ((r % 16) * 16 + r // 16)[:, None]).astype(BF16)
    return lperm, wc, pe, wflat


def kernel(x_prompt, x_sample, cache_nsa, cache_mla, state_win, page_table, c_prompt, c_sample,
           w_ada, b_ada, g_norm, w_in, w_cmp_k, w_cmp_v, pe_cmp_k, pe_cmp_v, rel_bias,
           g_q_lat, w_uq, g_kv_lat, w_uk, w_uv, w_out, g_final):
    assert w_in.shape[0] == 1, "one layer"
    nb, t, d = x_prompt.shape
    bd, ts, _ = x_sample.shape
    n_pages = page_table.shape[1]
    page = cache_nsa.shape[2]
    past = n_pages * page
    assert page == LANE and (past + ts) // CMP_STRIDE == past // CMP_STRIDE and ts <= SEL_BLOCK
    assert t % KT == 0 and (bd * ts) % TM == 0 and n_pages % PAGES_PER_STEP == 0
    wb = state_win.shape[2]

    wi = w_in[0]
    o1 = W_A
    o2 = o1 + 6 * G_A * HD_A
    o3 = o2 + 3 * H_A
    o4 = o3 + W_A
    o5 = o4 + Q_LORA
    o6 = o5 + KV_LORA
    o7 = o6 + ROPE_DIM
    wg = jnp.pad(wi[:, o2:o3], ((0, 0), (0, LANE - 3 * H_A)))
    wt = jnp.concatenate([wi[:, 0:o1] * (HD_A ** -0.5), wi[:, o3:o4], wi[:, o7:], wi[:, o4:o5], wg], axis=1).astype(BF16)
    wf = jnp.concatenate([wi[:, o1:o2], wi[:, o5:o6], wi[:, o6:o7]], axis=1).T.astype(BF16)
    wq = w_uq[0]
    wqn = wq[:, :, :NOPE].reshape(Q_LORA, H_B * NOPE).astype(BF16)
    wr = wq[:, :, NOPE:]
    half = ROPE_DIM // 2
    wsw = jnp.concatenate([-wr[..., half:], wr[..., :half]], axis=-1)
    padr = ((0, 0), (0, 0), (0, LANE - ROPE_DIM))
    wqr = jnp.pad(wr, padr).reshape(Q_LORA, H_B * LANE).astype(BF16)
    wqs = jnp.pad(wsw, padr).reshape(Q_LORA, H_B * LANE).astype(BF16)
    wuk = jnp.transpose(w_uk[0], (1, 2, 0)).astype(BF16)
    wuv = jnp.transpose(w_uv[0], (1, 0, 2)).astype(BF16)
    wts = (g_norm[0][None], wt, wf, g_q_lat[0][None], wqn, wqr, wqs, wuk, g_kv_lat[0][:, None])
    cw = _compress_weights(w_cmp_k[0], w_cmp_v[0], pe_cmp_k[0], pe_cmp_v[0])
    wout = w_out[0].astype(BF16)
    gfin = g_final[None]

    n_c = nb + bd
    n_cp = -(-n_c // 8) * 8
    c_all = jnp.pad(jnp.concatenate([c_prompt, c_sample], axis=0), ((0, n_cp - n_c), (0, 0)))
    mod = _modulation(c_all, w_ada[0].astype(BF16), b_ada)
    shift, scale, gate = mod[:, 0:d], mod[:, d:2 * d], mod[:, 2 * d:3 * d]

    def prompt_mod(a):
        return a[0:nb][:, None, :]

    def sample_mod(a):
        return jnp.repeat(a[nb:nb + bd], ts, axis=0).reshape(bd * ts // TM, TM, d)

    pos_p = jnp.arange(t, dtype=jnp.int32)
    xp = x_prompt.reshape(nb * t, d)
    (qa, gates, sza, szb, qm, nsat, wint, kvb, mlat, mlab) = _project(
        xp, prompt_mod(shift), prompt_mod(scale), False, nb, t, wts, *_rope_tables(pos_p))
    cmpk = _prompt_compress(kvb, cw)
    nch_p = t // CMP_STRIDE
    ns_p = t // SEL_BLOCK
    movt = _overlap_t(nch_p, ns_p).astype(BF16)
    emat = (jnp.arange(t)[None, :] // SEL_BLOCK == jnp.arange(ns_p)[:, None]).astype(BF16)
    oa = _prompt_nsa(rel_bias, qa, kvb, cmpk, gates, movt, emat)
    ol = _prompt_mla(qm, mlab)
    y_prompt = _merge(xp, prompt_mod(gate), False, t, oa, ol, sza, szb, wuv, wout, gfin).reshape(nb, t, d)

    n_s = bd * ts
    pos_s = past + jnp.arange(ts, dtype=jnp.int32)
    cos_tm, sin_tm, cos_t, sin_t = _rope_tables(pos_s)
    xs = x_sample.reshape(n_s, d)
    (qa_s, gates_s, sza_s, szb_s, qm_s, nsat_s, wint_s, _, mlat_s, _) = _project(
        xs, sample_mod(shift), sample_mod(scale), True, 1, n_s, wts,
        jnp.tile(cos_tm, (bd, 1)), jnp.tile(sin_tm, (bd, 1)), jnp.tile(cos_t, (1, bd)), jnp.tile(sin_t, (1, bd)))
    nrow = R_A * ts
    qa_g = qa_s.reshape(bd, ts, G_A, R_A, HD_A).transpose(0, 2, 3, 1, 4).reshape(bd, G_A, nrow, HD_A)
    gsel = gates_s[:, 0:3 * H_A].reshape(bd, ts, G_A, R_A, 3).transpose(0, 2, 3, 1, 4).reshape(bd, G_A, nrow, 3)
    gsel = jnp.pad(gsel, ((0, 0), (0, 0), (0, 0), (0, LANE - 3)))
    qm_h = qm_s.reshape(bd, ts, H_B, 2 * LANE).transpose(0, 2, 1, 3).reshape(bd, H_B * ts, 2 * LANE)
    kv_new = jnp.concatenate([nsat_s[0], wint_s[0]], axis=0).reshape(6 * LANE, bd, ts).transpose(1, 0, 2)
    kv_new = jnp.pad(kv_new, ((0, 0), (0, 0), (0, LANE - ts)))
    mla_new = jnp.pad(mlat_s[0].reshape(KV_LORA + ROPE_DIM, bd, ts).transpose(1, 0, 2), ((0, 0), (0, 0), (0, LANE - ts)))
    n_phys = cache_nsa.shape[1]
    cache_nsa_t = jnp.transpose(cache_nsa[0], (0, 2, 3, 4, 1)).reshape(n_phys, 4 * G_A * HD_A, page)
    cache_mla_t = jnp.transpose(cache_mla[0], (0, 2, 1))
    win_t = jnp.transpose(state_win[0], (0, 2, 3, 4, 1)).reshape(bd, 2 * G_A * HD_A, wb)
    nch_s = past // CMP_STRIDE
    nbp = past // SEL_BLOCK
    nsp = -(-(nbp + 1) // LANE) * LANE
    mov_s = jnp.pad(_overlap_t(nch_s, nbp + 1).T, ((0, 0), (0, nsp - nbp - 1))).astype(BF16)
    emat_s = (jnp.arange(past)[None, :] // SEL_BLOCK == jnp.arange(nbp)[:, None]).astype(BF16)
    oa_g, win_new = _decode_nsa(page_table, rel_bias, cache_nsa_t, qa_g, kv_new, win_t, gsel, cw, mov_s, emat_s, ts, past)
    ol_h = _decode_mla(page_table, cache_mla_t, qm_h, mla_new, ts)
    oa_s = oa_g.reshape(bd, G_A, R_A, ts, HD_A).transpose(0, 3, 1, 2, 4).reshape(n_s, W_A)
    ol_s = ol_h.reshape(bd, H_B, ts, KV_LORA).transpose(0, 2, 1, 3).reshape(n_s, H_B * KV_LORA)
    y_sample = _merge(xs, sample_mod(gate), True, n_s, oa_s, ol_s, sza_s, szb_s, wuv, wout, gfin).reshape(bd, ts, d)

    new_nsa_prompt = nsat.reshape(nb, 4, G_A, HD_A, t).transpose(0, 4, 1, 2, 3)[None]
    new_nsa_sample = nsat_s[0].reshape(4, G_A, HD_A, bd, ts).transpose(3, 4, 0, 1, 2)[None]
    new_mla_prompt = mlat.transpose(0, 2, 1)[None]
    new_mla_sample = mlat_s[0].reshape(KV_LORA + ROPE_DIM, bd, ts).transpose(1, 2, 0)[None]
    wlen = min(WINDOW, t)
    new_win_prompt = wint[:, :, t - wlen:].reshape(nb, 2, G_A, HD_A, wlen).transpose(0, 4, 1, 2, 3)[None]
    new_win_sample = win_new.reshape(bd, 2, G_A, HD_A, wb).transpose(0, 4, 1, 2, 3)[None]
    return (y_prompt, y_sample, new_nsa_prompt, new_nsa_sample, new_mla_prompt, new_mla_sample,
            new_win_prompt, new_win_sample)
```

```python
import functools
import math

import jax
import jax.numpy as jnp
from jax import lax
from jax.experimental import pallas as pl
from jax.experimental.pallas import tpu as pltpu

F32 = jnp.float32
BF16 = jnp.bfloat16

H_A, G_A, R_A, HD_A = 8, 2, 4, 64
W_A = H_A * HD_A
CMP_BLOCK, CMP_STRIDE = 32, 16
SEL_BLOCK, N_SEL = 64, 16
WINDOW = 512
FORCE_SCORE = 1e4
H_B, NOPE, ROPE_DIM, V_HEAD, Q_LORA, KV_LORA = 4, 128, 64, 128, 256, 128
W_B = H_B * V_HEAD
ROPE_BASE = 10000.0
N_BUCKETS, MAX_DIST = 32, 128
EPS = 1e-6
NEG = -1e30

TQ = 128
TQM = 256
KT = 1024
TM = 256
LANE = 128
PAGES_PER_STEP = 64
MLA_PAGE_GROUP = 16
VMEM_LIMIT = 56 * 1024 * 1024


def _dot(a, b):
    return jnp.dot(a, b, preferred_element_type=F32)


def _dot_nt(a, b):
    return lax.dot_general(a, b, (((1,), (1,)), ((), ())), preferred_element_type=F32)


def _rel_bucket(dist):
    n = jnp.maximum(dist, 0)
    max_exact = N_BUCKETS // 2
    span = N_BUCKETS - max_exact
    large = jnp.full(n.shape, max_exact, jnp.int32)
    for k in range(1, span):
        thr = math.ceil(max_exact * (MAX_DIST / max_exact) ** (k / span))
        large = large + (n >= thr).astype(jnp.int32)
    return jnp.where(n < max_exact, n, large)


def _bias_lookup(bucket, tab_ref, h):
    acc = jnp.zeros(bucket.shape, F32)
    for k in range(N_BUCKETS):
        acc = jnp.where(bucket == k, tab_ref[k, h], acc)
    return acc


def _softmax_rows(s):
    m = jnp.max(s, axis=-1, keepdims=True)
    p = jnp.exp(s - m)
    l = jnp.sum(p, axis=-1, keepdims=True)
    return jnp.where(s > 0.5 * NEG, p / l, 0.0)


def _mod_kernel(c_ref, w_ref, b_ref, o_ref):
    o_ref[...] = _dot(c_ref[...].astype(BF16), w_ref[...]) + b_ref[...]


def _modulation(c, w_ada_b, b_ada):
    n, d = c.shape
    n3 = w_ada_b.shape[1]
    tn = 1024
    return pl.pallas_call(
        _mod_kernel,
        grid=(n3 // tn,),
        in_specs=[pl.BlockSpec((n, d), lambda j: (0, 0)),
                  pl.BlockSpec((d, tn), lambda j: (0, j)),
                  pl.BlockSpec((1, tn), lambda j: (0, j))],
        out_specs=pl.BlockSpec((n, tn), lambda j: (0, j)),
        out_shape=jax.ShapeDtypeStruct((n, n3), F32),
        name="adaln_mod",
    )(c, w_ada_b, b_ada)


def _proj_kernel(x_ref, shift_ref, scale_ref, gn_ref, wt_ref, wf_ref, gq_ref, wqn_ref, wqr_ref,
                 wqs_ref, wuk_ref, gkv_ref, cos_ref, sin_ref, cost_ref, sint_ref,
                 qa_ref, gates_ref, sza_ref, szb_ref, qm_ref, nsat_ref, wint_ref, kvb_ref,
                 mlat_ref, mlab_ref, *, mla_scale):
    x = x_ref[...]
    y = x * lax.rsqrt(jnp.mean(x * x, axis=-1, keepdims=True) + EPS) * gn_ref[...]
    h = y * (1.0 + scale_ref[0]) + shift_ref[0]
    hb = h.astype(BF16)

    ut = _dot(hb, wt_ref[...])
    qa_ref[...] = ut[:, 0:W_A].astype(BF16)
    za = ut[:, W_A:2 * W_A]
    zb = ut[:, 2 * W_A:3 * W_A]
    sza_ref[...] = za * jax.nn.sigmoid(za)
    szb_ref[...] = zb * jax.nn.sigmoid(zb)
    cq = ut[:, 3 * W_A:3 * W_A + Q_LORA]
    gates_ref[...] = jax.nn.sigmoid(ut[:, 3 * W_A + Q_LORA:3 * W_A + Q_LORA + LANE])

    cqn = (cq * lax.rsqrt(jnp.mean(cq * cq, axis=-1, keepdims=True) + EPS) * gq_ref[...]).astype(BF16)
    qn = _dot(cqn, wqn_ref[...])
    qr = _dot(cqn, wqr_ref[...])
    qs = _dot(cqn, wqs_ref[...])
    q_rope = qr * cos_ref[...] + qs * sin_ref[...]
    for hh in range(H_B):
        q_lat = _dot(qn[:, hh * NOPE:(hh + 1) * NOPE].astype(BF16), wuk_ref[hh])
        qm_ref[:, 2 * LANE * hh:2 * LANE * hh + LANE] = (q_lat * mla_scale).astype(BF16)
        qm_ref[:, 2 * LANE * hh + LANE:2 * LANE * (hh + 1)] = (
            q_rope[:, hh * LANE:(hh + 1) * LANE] * mla_scale).astype(BF16)

    uf = _dot_nt(wf_ref[...], hb)
    n_sel_rows = 4 * G_A * HD_A
    n_kv_rows = 6 * G_A * HD_A
    nsat_ref[0] = uf[0:n_sel_rows]
    wint_ref[0] = uf[n_sel_rows:n_kv_rows]
    kvb_ref[0] = uf[0:n_kv_rows].astype(BF16)
    c = uf[n_kv_rows:n_kv_rows + KV_LORA]
    cn = c * lax.rsqrt(jnp.mean(c * c, axis=0, keepdims=True) + EPS) * gkv_ref[...]
    kr = uf[n_kv_rows + KV_LORA:n_kv_rows + KV_LORA + ROPE_DIM]
    half = ROPE_DIM // 2
    x1, x2 = kr[0:half], kr[half:ROPE_DIM]
    ct, st = cost_ref[...], sint_ref[...]
    mla = jnp.concatenate([cn, x1 * ct - x2 * st, x1 * st + x2 * ct], axis=0)
    mlat_ref[0] = mla
    n_pad = 2 * LANE - KV_LORA - ROPE_DIM
    pad = jnp.where(lax.broadcasted_iota(jnp.int32, (n_pad, mla.shape[1]), 0) == 0, 1.0, 0.0)
    mlab_ref[0] = jnp.concatenate([mla, pad], axis=0).astype(BF16)


def _project(xf, shift3, scale3, per_row_mod, nb, tb, wts, cos_tm, sin_tm, cos_t, sin_t):
    n, d = xf.shape
    tiles_b = tb // TM
    ntab = cos_tm.shape[0] // TM
    (gn, wt, wf, gq, wqn, wqr, wqs, wuk, gkv) = wts
    if per_row_mod:
        mod_spec = pl.BlockSpec((1, TM, d), lambda r: (r, 0, 0))
    else:
        mod_spec = pl.BlockSpec((1, 1, d), lambda r: (r // tiles_b, 0, 0))

    def full(a):
        return pl.BlockSpec(a.shape, lambda r, _n=a.ndim: (0,) * _n)

    def fm(rows):
        return pl.BlockSpec((1, rows, TM), lambda r: (r // tiles_b, 0, r % tiles_b))

    def tmaj(cols):
        return pl.BlockSpec((TM, cols), lambda r: (r, 0))

    n_sel_rows = 4 * G_A * HD_A
    out_shape = (
        jax.ShapeDtypeStruct((n, W_A), BF16),
        jax.ShapeDtypeStruct((n, LANE), F32),
        jax.ShapeDtypeStruct((n, W_A), F32),
        jax.ShapeDtypeStruct((n, W_B), F32),
        jax.ShapeDtypeStruct((n, H_B * 2 * LANE), BF16),
        jax.ShapeDtypeStruct((nb, n_sel_rows, tb), F32),
        jax.ShapeDtypeStruct((nb, 2 * G_A * HD_A, tb), F32),
        jax.ShapeDtypeStruct((nb, 6 * G_A * HD_A, tb), BF16),
        jax.ShapeDtypeStruct((nb, KV_LORA + ROPE_DIM, tb), F32),
        jax.ShapeDtypeStruct((nb, 2 * LANE, tb), BF16),
    )
    out_specs = (tmaj(W_A), tmaj(LANE), tmaj(W_A), tmaj(W_B), tmaj(H_B * 2 * LANE),
                 fm(n_sel_rows), fm(2 * G_A * HD_A), fm(6 * G_A * HD_A), fm(KV_LORA + ROPE_DIM), fm(2 * LANE))
    in_specs = [tmaj(d), mod_spec, mod_spec, full(gn), full(wt), full(wf), full(gq), full(wqn),
                full(wqr), full(wqs), full(wuk), full(gkv),
                pl.BlockSpec((TM, H_B * LANE), lambda r: (r % ntab, 0)),
                pl.BlockSpec((TM, H_B * LANE), lambda r: (r % ntab, 0)),
                pl.BlockSpec((ROPE_DIM // 2, TM), lambda r: (0, r % ntab)),
                pl.BlockSpec((ROPE_DIM // 2, TM), lambda r: (0, r % ntab))]
    return pl.pallas_call(
        functools.partial(_proj_kernel, mla_scale=(NOPE + ROPE_DIM) ** -0.5),
        grid=(n // TM,),
        in_specs=in_specs,
        out_specs=out_specs,
        out_shape=out_shape,
        compiler_params=pltpu.CompilerParams(vmem_limit_bytes=VMEM_LIMIT),
        name="in_proj",
    )(xf, shift3, scale3, gn, wt, wf, gq, wqn, wqr, wqs, wuk, gkv, cos_tm, sin_tm, cos_t, sin_t)


def _chunk_rows(lperm_ref, feats_tok):
    return _dot_nt(lperm_ref[...], feats_tok).astype(BF16)


def _compress_chunks(a_ref, wc_ref, pe_ref, wflat_ref, nch):
    outs = []
    for kind in range(2):
        acc = jnp.zeros((nch, 2 * LANE), F32)
        for p in range(CMP_STRIDE // 2):
            lhs = jnp.concatenate([a_ref[2 * p, :, kind * LANE:(kind + 1) * LANE],
                                   a_ref[2 * p + 1, :, kind * LANE:(kind + 1) * LANE]], axis=1)
            acc = acc + _dot(lhs, wc_ref[kind, p])
        first, second = acc[:, 0:LANE], acc[:, LANE:2 * LANE]
        cb = _dot(pe_ref[kind], wflat_ref[kind])[0:1]
        outs.append(first + pltpu.roll(second, nch - 1, 0) + cb)
    return jnp.concatenate(outs, axis=1)


def _prompt_compress_kernel(kv_ref, lperm_ref, wc_ref, pe_ref, wflat_ref, o_ref, a_ref):
    t = kv_ref.shape[2]
    nch = t // CMP_STRIDE
    for tix in range(t // 256):
        tb = _chunk_rows(lperm_ref, kv_ref[0, :, tix * 256:(tix + 1) * 256])
        for j in range(CMP_STRIDE):
            a_ref[j, tix * 16:(tix + 1) * 16, :] = tb[16 * j:16 * (j + 1)]
    o_ref[0] = _compress_chunks(a_ref, wc_ref, pe_ref, wflat_ref, nch).astype(BF16)


def _prompt_compress(kvb, cw):
    nb, _, t = kvb.shape
    nch = t // CMP_STRIDE
    lperm, wc, pe, wflat = cw

    def full(a):
        return pl.BlockSpec(a.shape, lambda b, _n=a.ndim: (0,) * _n)

    return pl.pallas_call(
        _prompt_compress_kernel,
        grid=(nb,),
        in_specs=[pl.BlockSpec((1, 2 * LANE, t), lambda b: (b, 0, 0)), full(lperm), full(wc), full(pe), full(wflat)],
        out_specs=pl.BlockSpec((1, nch, 2 * LANE), lambda b: (b, 0, 0)),
        out_shape=jax.ShapeDtypeStruct((nb, nch, 2 * LANE), BF16),
        scratch_shapes=[pltpu.VMEM((CMP_STRIDE, nch, 2 * LANE), BF16)],
        compiler_params=pltpu.CompilerParams(vmem_limit_bytes=VMEM_LIMIT),
        name="prompt_compress",
    )(kvb, lperm, wc, pe, wflat)


def _rank(score, n_valid, axis):
    size = 8 if axis == 0 else LANE
    ntile = score.shape[axis] // size

    def tile(a, k):
        return a[k * size:(k + 1) * size, :] if axis == 0 else a[:, k * size:(k + 1) * size]

    tiles = [tile(score, k) for k in range(ntile)]
    ranks = [jnp.zeros(tiles[0].shape, F32) for _ in range(ntile)]
    pos = lax.broadcasted_iota(jnp.int32, tiles[0].shape, axis)
    for j2 in range(n_valid):
        kt, off = divmod(j2, size)
        cand = tiles[kt][off:off + 1, :] if axis == 0 else tiles[kt][:, off:off + 1]
        for k in range(ntile):
            if k > kt:
                before = cand >= tiles[k]
            elif k < kt:
                before = cand > tiles[k]
            else:
                before = (cand > tiles[k]) | ((cand == tiles[k]) & (pos > off))
            ranks[k] = ranks[k] + jnp.where(before, 1.0, 0.0)
    return jnp.concatenate(ranks, axis=axis)


def _prompt_nsa_kernel(tab_ref, qa_ref, sel_ref, win_ref, cmp_ref, gates_ref, movt_ref, eneg_ref,
                       o_ref, bwin_ref, dsel_ref, dcmp_ref, sc_ref):
    i = pl.program_id(0)
    b = pl.program_id(1)
    t = sel_ref.shape[2]
    nch = cmp_ref.shape[1]
    nc = nch - 1
    ns = movt_ref.shape[0]
    n_win_tiles = WINDOW // TQ + 1

    @pl.when((i == 0) & (b == 0))
    def _():
        tl = lax.broadcasted_iota(jnp.int32, (TQ, n_win_tiles * TQ), 0)
        kl = lax.broadcasted_iota(jnp.int32, (TQ, n_win_tiles * TQ), 1)
        dist = tl + WINDOW - kl
        bucket = _rel_bucket(dist)
        ok = (dist >= 0) & (dist < WINDOW)
        tl2 = lax.broadcasted_iota(jnp.int32, (TQ, 2 * TQ), 0)
        kl2 = lax.broadcasted_iota(jnp.int32, (TQ, 2 * TQ), 1)
        dist2 = tl2 + TQ - kl2
        bucket2 = _rel_bucket(dist2)
        for h in range(H_A):
            bwin_ref[h] = jnp.where(ok, _bias_lookup(bucket, tab_ref, h), NEG)
            dsel_ref[h] = jnp.where(dist2 >= 0,
                                    _bias_lookup(bucket2, tab_ref, h) - tab_ref[N_BUCKETS - 1, h], NEG)

    @pl.when(b == 0)
    def _():
        tl = lax.broadcasted_iota(jnp.int32, (TQ, nch), 0)
        n = lax.broadcasted_iota(jnp.int32, (TQ, nch), 1)
        dist = i * TQ + tl - (n * CMP_STRIDE + CMP_BLOCK - 1)
        bucket = _rel_bucket(dist)
        ok = (dist >= 0) & (n < nc)
        for h in range(H_A):
            dcmp_ref[h] = jnp.where(ok, _bias_lookup(bucket, tab_ref, h) - tab_ref[N_BUCKETS - 1, h], NEG)

    i_prev = jnp.maximum(i - 1, 0)
    n_main = (i_prev * TQ + KT - 1) // KT
    c_a = pl.multiple_of(i_prev * TQ, TQ)
    c_b = pl.multiple_of(i * TQ, TQ)
    blocks_per_tile = TQ // SEL_BLOCK
    ones_rows = jnp.ones((16, KT), BF16)
    gates = gates_ref[...]

    def k_ext(g, c0, width):
        return jnp.concatenate([sel_ref[0, HD_A * g:HD_A * (g + 1), pl.ds(c0, width)],
                                eneg_ref[:, pl.ds(c0, width)]], axis=0)

    def v_ext(g, c0, width):
        return jnp.concatenate([sel_ref[0, LANE + HD_A * g:LANE + HD_A * (g + 1), pl.ds(c0, width)],
                                ones_rows[:, 0:width]], axis=0)

    q2s, o_cmps, lhs_mains, lhs_tails = [], [], [], []
    for g in range(G_A):
        q2 = jnp.concatenate([qa_ref[:, HD_A * (R_A * g + r):HD_A * (R_A * g + r + 1)] for r in range(R_A)],
                             axis=0)
        kc = cmp_ref[0, :, HD_A * g:HD_A * (g + 1)]
        vc = cmp_ref[0, :, LANE + HD_A * g:LANE + HD_A * (g + 1)]
        s = _dot_nt(q2, kc).reshape(R_A, TQ, nch) + dcmp_ref[R_A * g:R_A * (g + 1)]
        p_c = _softmax_rows(s.reshape(R_A * TQ, nch))
        p_cb = p_c.astype(BF16)
        o_cmp = _dot(p_cb, vc)
        imp_t = jnp.zeros((ns, TQ), F32)
        for r in range(R_A):
            imp_t = imp_t + _dot_nt(movt_ref[...], p_cb[r * TQ:(r + 1) * TQ])
        jj = lax.broadcasted_iota(jnp.int32, (ns, TQ), 0)
        cc = (i * TQ + lax.broadcasted_iota(jnp.int32, (ns, TQ), 1)) // SEL_BLOCK
        forced = (jj == 0) | (jj == cc) | (jj == cc - 1)
        score = jnp.where(forced, FORCE_SCORE, jnp.where(jj <= cc, imp_t, -jnp.inf))
        rank = _rank(score, ns, 0)
        unsel_t = jnp.where((rank < min(N_SEL, ns)) & (jj <= cc), 0.0, 1.0)
        unsel_main = jnp.transpose(jnp.where(jj < blocks_per_tile * (i - 1), unsel_t, 1.0)).astype(BF16)
        unsel_tail = jnp.transpose(unsel_t).astype(BF16)

        q2s.append(q2)
        o_cmps.append(o_cmp)
        lhs_mains.append(jnp.concatenate([q2, jnp.concatenate([unsel_main] * R_A, axis=0)], axis=1))
        lhs_tails.append(jnp.concatenate([q2, jnp.concatenate([unsel_tail] * R_A, axis=0)], axis=1))

    def score_tile(kt, ms):
        c0 = pl.multiple_of(kt * KT, KT)
        out = []
        for g in range(G_A):
            s = _dot(lhs_mains[g], k_ext(g, c0, KT))
            sc_ref[g, :, pl.ds(c0, KT)] = s
            out.append(jnp.maximum(ms[g], jnp.max(s, axis=-1, keepdims=True)))
        return tuple(out)

    ms = lax.fori_loop(0, n_main, score_tile, tuple(jnp.full((R_A * TQ, 1), NEG, F32) for _ in range(G_A)))
    tails = []
    for g in range(G_A):
        d_here = dsel_ref[R_A * g:R_A * (g + 1)].reshape(R_A * TQ, 2 * TQ)
        s_a = _dot(lhs_tails[g], k_ext(g, c_a, TQ)) + d_here[:, 0:TQ] + jnp.where(i == 0, NEG, 0.0)
        s_b = _dot(lhs_tails[g], k_ext(g, c_b, TQ)) + d_here[:, TQ:2 * TQ]
        m = jnp.maximum(ms[g], jnp.maximum(jnp.max(s_a, axis=-1, keepdims=True),
                                           jnp.max(s_b, axis=-1, keepdims=True)))
        tails.append((s_a, s_b, m))

    def pv_tile(kt, accs):
        c0 = pl.multiple_of(kt * KT, KT)
        out = []
        for g in range(G_A):
            p = jnp.exp(sc_ref[g, :, pl.ds(c0, KT)] - tails[g][2]).astype(BF16)
            out.append(accs[g] + _dot_nt(p, v_ext(g, c0, KT)))
        return tuple(out)

    accs = lax.fori_loop(0, n_main, pv_tile,
                         tuple(jnp.zeros((R_A * TQ, HD_A + 16), F32) for _ in range(G_A)))

    for g in range(G_A):
        q2, o_cmp = q2s[g], o_cmps[g]
        s_a, s_b, m = tails[g]
        acc = (accs[g] + _dot_nt(jnp.exp(s_a - m).astype(BF16), v_ext(g, c_a, TQ))
               + _dot_nt(jnp.exp(s_b - m).astype(BF16), v_ext(g, c_b, TQ)))
        o_sel = acc[:, 0:HD_A] / acc[:, HD_A:HD_A + 1]

        s_w = []
        for w in range(n_win_tiles):
            kt = i - (n_win_tiles - 1) + w
            c0 = pl.multiple_of(jnp.maximum(kt, 0) * TQ, TQ)
            kw = win_ref[0, HD_A * g:HD_A * (g + 1), pl.ds(c0, TQ)]
            bw = bwin_ref[R_A * g:R_A * (g + 1), :, w * TQ:(w + 1) * TQ].reshape(R_A * TQ, TQ)
            s_w.append(_dot(q2, kw) + bw + jnp.where(kt < 0, NEG, 0.0))
        s_w = jnp.concatenate(s_w, axis=1)
        p_w = jnp.exp(s_w - jnp.max(s_w, axis=-1, keepdims=True)).astype(BF16)
        acc_w = jnp.zeros((R_A * TQ, HD_A + 16), F32)
        for w in range(n_win_tiles):
            kt = i - (n_win_tiles - 1) + w
            c0 = pl.multiple_of(jnp.maximum(kt, 0) * TQ, TQ)
            vw = jnp.concatenate([win_ref[0, LANE + HD_A * g:LANE + HD_A * (g + 1), pl.ds(c0, TQ)],
                                  ones_rows[:, 0:TQ]], axis=0)
            acc_w = acc_w + _dot_nt(p_w[:, w * TQ:(w + 1) * TQ], vw)
        o_win = acc_w[:, 0:HD_A] / acc_w[:, HD_A:HD_A + 1]

        for r in range(R_A):
            h = R_A * g + r
            rows = slice(r * TQ, (r + 1) * TQ)
            o_ref[:, HD_A * h:HD_A * (h + 1)] = (gates[:, 3 * h:3 * h + 1] * o_cmp[rows]
                                                  + gates[:, 3 * h + 1:3 * h + 2] * o_sel[rows]
                                                  + gates[:, 3 * h + 2:3 * h + 3] * o_win[rows])


def _prompt_nsa(rel_bias, qa, kvb, cmpk, gates, movt, emat):
    nb, _, t = kvb.shape
    ni = t // TQ
    nch = cmpk.shape[1]
    n_win_tiles = WINDOW // TQ + 1

    def full(a):
        return pl.BlockSpec(a.shape, lambda i, b, _n=a.ndim: (0,) * _n)

    return pl.pallas_call(
        _prompt_nsa_kernel,
        grid=(ni, nb),
        in_specs=[pl.BlockSpec(memory_space=pltpu.SMEM),
                  pl.BlockSpec((TQ, W_A), lambda i, b: (b * ni + i, 0)),
                  pl.BlockSpec((1, 2 * LANE, t), lambda i, b: (b, 1, 0)),
                  pl.BlockSpec((1, 2 * LANE, t), lambda i, b: (b, 2, 0)),
                  pl.BlockSpec((1, nch, 2 * LANE), lambda i, b: (b, 0, 0)),
                  pl.BlockSpec((TQ, LANE), lambda i, b: (b * ni + i, 0)),
                  full(movt), full(emat)],
        out_specs=pl.BlockSpec((TQ, W_A), lambda i, b: (b * ni + i, 0)),
        out_shape=jax.ShapeDtypeStruct((nb * t, W_A), F32),
        scratch_shapes=[pltpu.VMEM((H_A, TQ, n_win_tiles * TQ), F32),
                        pltpu.VMEM((H_A, TQ, 2 * TQ), F32),
                        pltpu.VMEM((H_A, TQ, nch), F32),
                        pltpu.VMEM((G_A, R_A * TQ, t), F32)],
        compiler_params=pltpu.CompilerParams(vmem_limit_bytes=VMEM_LIMIT,
                                             dimension_semantics=("arbitrary", "arbitrary")),
        name="prompt_nsa",
    )(rel_bias, qa, kvb, kvb, cmpk, gates, movt, emat)


def _prompt_mla_kernel(qm_ref, mla_ref, o_ref, sc_ref, acc_ref):
    i = pl.program_id(1)
    nrow = H_B * TQM
    n_full = (i * TQM) // KT
    q2 = jnp.concatenate([qm_ref[:, 2 * LANE * h:2 * LANE * (h + 1)] for h in range(H_B)], axis=0)

    def score_tile(kt, m):
        c0 = pl.multiple_of(kt * KT, KT)
        s = _dot(q2, mla_ref[0, :, pl.ds(c0, KT)])
        sc_ref[:, pl.ds(c0, KT)] = s
        return jnp.maximum(m, jnp.max(s, axis=-1, keepdims=True))

    m = lax.fori_loop(0, n_full, score_tile, jnp.full((nrow, 1), NEG, F32))
    c_d = pl.multiple_of(n_full * KT, KT)
    qpos = i * TQM + lax.broadcasted_iota(jnp.int32, (H_B, TQM, KT), 1).reshape(nrow, KT)
    kpos = c_d + lax.broadcasted_iota(jnp.int32, (nrow, KT), 1)
    s_d = jnp.where(kpos <= qpos, _dot(q2, mla_ref[0, :, pl.ds(c_d, KT)]), NEG)
    m = jnp.maximum(m, jnp.max(s_d, axis=-1, keepdims=True))
    acc_ref[...] = _dot_nt(jnp.exp(s_d - m).astype(BF16), mla_ref[0, :, pl.ds(c_d, KT)])

    def pv_tile(kt, carry):
        c0 = pl.multiple_of(kt * KT, KT)
        p = jnp.exp(sc_ref[:, pl.ds(c0, KT)] - m).astype(BF16)
        acc_ref[...] += _dot_nt(p, mla_ref[0, :, pl.ds(c0, KT)])
        return carry

    lax.fori_loop(0, n_full, pv_tile, 0)
    acc = acc_ref[...]
    o = acc[:, 0:KV_LORA] / acc[:, KV_LORA + ROPE_DIM:KV_LORA + ROPE_DIM + 1]
    for h in range(H_B):
        o_ref[:, KV_LORA * h:KV_LORA * (h + 1)] = o[h * TQM:(h + 1) * TQM]


def _prompt_mla(qm, mlab):
    nb, rows, t = mlab.shape
    ni = t // TQM
    return pl.pallas_call(
        _prompt_mla_kernel,
        grid=(nb, ni),
        in_specs=[pl.BlockSpec((TQM, H_B * 2 * LANE), lambda b, i: (b * ni + i, 0)),
                  pl.BlockSpec((1, rows, t), lambda b, i: (b, 0, 0))],
        out_specs=pl.BlockSpec((TQM, H_B * KV_LORA), lambda b, i: (b * ni + i, 0)),
        out_shape=jax.ShapeDtypeStruct((nb * t, H_B * KV_LORA), F32),
        scratch_shapes=[pltpu.VMEM((H_B * TQM, t), F32), pltpu.VMEM((H_B * TQM, rows), F32)],
        compiler_params=pltpu.CompilerParams(vmem_limit_bytes=VMEM_LIMIT),
        name="prompt_mla",
    )(qm, mlab)


def _merge_kernel(x_ref, gate_ref, oa_ref, ol_ref, sza_ref, szb_ref, wuv_ref, wout_ref, gf_ref, y_ref):
    ya = oa_ref[...] * sza_ref[...]
    ol = ol_ref[...]
    szb = szb_ref[...]
    yb = [_dot(ol[:, KV_LORA * h:KV_LORA * (h + 1)].astype(BF16), wuv_ref[h]) * szb[:, V_HEAD * h:V_HEAD * (h + 1)]
          for h in range(H_B)]
    yab = jnp.concatenate([ya] + yb, axis=1).astype(BF16)
    xn = x_ref[...] + gate_ref[0] * _dot(yab, wout_ref[...])
    y_ref[...] = xn * lax.rsqrt(jnp.mean(xn * xn, axis=-1, keepdims=True) + EPS) * gf_ref[...]


def _merge(xf, gate3, per_row_mod, tb, oa, ol, sza, szb, wuv, wout, gfin):
    n, d = xf.shape
    tiles_b = tb // TM
    if per_row_mod:
        mod_spec = pl.BlockSpec((1, TM, d), lambda r: (r, 0, 0))
    else:
        mod_spec = pl.BlockSpec((1, 1, d), lambda r: (r // tiles_b, 0, 0))

    def full(a):
        return pl.BlockSpec(a.shape, lambda r, _n=a.ndim: (0,) * _n)

    def tmaj(cols):
        return pl.BlockSpec((TM, cols), lambda r: (r, 0))

    return pl.pallas_call(
        _merge_kernel,
        grid=(n // TM,),
        in_specs=[tmaj(d), mod_spec, tmaj(W_A), tmaj(W_B), tmaj(W_A), tmaj(W_B), full(wuv), full(wout), full(gfin)],
        out_specs=tmaj(d),
        out_shape=jax.ShapeDtypeStruct((n, d), F32),
        compiler_params=pltpu.CompilerParams(vmem_limit_bytes=VMEM_LIMIT),
        name="out_merge",
    )(xf, gate3, oa, ol, sza, szb, wuv, wout, gfin)


def _row_head_bias(bucket, tab_ref, g, shift):
    ts = bucket.shape[0] // R_A
    row_r = lax.broadcasted_iota(jnp.int32, bucket.shape, 0) // ts
    out = jnp.zeros(bucket.shape, F32)
    for r in range(R_A):
        h = R_A * g + r
        val = _bias_lookup(bucket, tab_ref, h)
        if shift:
            val = val - tab_ref[N_BUCKETS - 1, h]
        out = jnp.where(row_r == r, val, out)
    return out


def _own_lanes(x, b, ts):
    per_block = LANE // ts
    shift = (LANE - ts * (b % per_block)) % LANE
    lane = lax.broadcasted_iota(jnp.int32, x.shape, 1)
    return jnp.where(lane < ts, pltpu.roll(x, shift, 1), 0.0)


def _decode_nsa_kernel(pt_ref, tab_ref, *refs, ts, past, pps):
    pages = refs[:pps]
    (qa_ref, nsel_ref, nwin_ref, win_ref, gsel_ref, lperm_ref, wc_ref, pe_ref, wflat_ref, mov_ref, e_ref,
     o_ref, wout_ref, a_ref, stash_ref, bc_ref, ds_ref, dn_ref, bw_ref, bwn_ref) = refs[pps:]
    del pt_ref
    b = pl.program_id(0)
    s = pl.program_id(1)
    n_steps = pl.num_programs(1)
    nch = past // CMP_STRIDE
    nc = nch - 1
    nrow = R_A * ts
    n_cmp_rows = 2 * G_A * HD_A
    wb = win_ref.shape[2]

    for pp in range(pps // 2):
        pa = pages[2 * pp][0]
        pb = pages[2 * pp + 1][0]
        pair = s * (pps // 2) + pp
        both = jnp.concatenate([pa[0:n_cmp_rows], pb[0:n_cmp_rows]], axis=1).astype(BF16)
        tb = _chunk_rows(lperm_ref, both)
        c0 = pl.multiple_of(pair * 16, 16)
        for j in range(CMP_STRIDE):
            a_ref[j, pl.ds(c0, 16), :] = tb[16 * j:16 * (j + 1)]
        k0 = pl.multiple_of(pair * 256, 256)
        stash_ref[:, pl.ds(k0, 256)] = jnp.concatenate(
            [pa[n_cmp_rows:2 * n_cmp_rows], pb[n_cmp_rows:2 * n_cmp_rows]], axis=1).astype(BF16)

    @pl.when(s == n_steps - 1)
    def _():
        trow = lax.broadcasted_iota(jnp.int32, (nrow, 1), 0) % ts

        @pl.when(b == 0)
        def _():
            n = lax.broadcasted_iota(jnp.int32, (nrow, nch), 1)
            dist = past + trow - (n * CMP_STRIDE + CMP_BLOCK - 1)
            u = lax.broadcasted_iota(jnp.int32, (nrow, LANE), 1)
            dist_s = LANE + trow - u
            dist_n = trow - u
            kl = lax.broadcasted_iota(jnp.int32, (nrow, wb), 1)
            dist_w = wb + trow - kl
            ok_n = (dist_n >= 0) & (u < ts)
            for g in range(G_A):
                bc_ref[g] = jnp.where((dist >= 0) & (n < nc), _row_head_bias(_rel_bucket(dist), tab_ref, g, False), NEG)
                ds_ref[g] = _row_head_bias(_rel_bucket(dist_s), tab_ref, g, True)
                dn_ref[g] = jnp.where(ok_n, _row_head_bias(_rel_bucket(dist_n), tab_ref, g, True), NEG)
                bw_ref[g] = jnp.where(dist_w < WINDOW, _row_head_bias(_rel_bucket(dist_w), tab_ref, g, False), NEG)
                bwn_ref[g] = jnp.where(ok_n, _row_head_bias(_rel_bucket(dist_n), tab_ref, g, False), NEG)

        cmpk = _compress_chunks(a_ref, wc_ref, pe_ref, wflat_ref, nch).astype(BF16)
        new_sel = _own_lanes(nsel_ref[0], b, ts)
        new_win = _own_lanes(nwin_ref[0], b, ts)
        nsp = mov_ref.shape[1]
        nbp = past // SEL_BLOCK
        jj = lax.broadcasted_iota(jnp.int32, (nrow, nsp), 1)
        cc = (past + trow) // SEL_BLOCK
        o_cmps, tots = [], []
        for g in range(G_A):
            q2 = qa_ref[0, g]
            s_c = _dot_nt(q2, cmpk[:, HD_A * g:HD_A * (g + 1)]) + bc_ref[g]
            p_c = _softmax_rows(s_c).astype(BF16)
            o_cmps.append(_dot(p_c, cmpk[:, LANE + HD_A * g:LANE + HD_A * (g + 1)]))
            imp = _dot(p_c, mov_ref[...])
            tot = imp
            for r in range(1, R_A):
                tot = tot + pltpu.roll(imp, r * ts, 0)
            tots.append(tot)
        upper = lax.broadcasted_iota(jnp.int32, (nrow, nsp), 0) < nrow // 2
        forced = (jj == 0) | (jj == cc) | (jj == cc - 1)
        score = jnp.where(forced, FORCE_SCORE, jnp.where(jj <= cc, jnp.where(upper, tots[0], tots[1]), -jnp.inf))
        rank = _rank(score, nbp + 1, 1)
        rank_sw = pltpu.roll(rank, nrow // 2, 0)
        allowed = jj <= cc
        sels = [((jnp.where(upper, rank, rank_sw) < N_SEL) & allowed).astype(BF16)[:, 0:nbp],
                ((jnp.where(upper, rank_sw, rank) < N_SEL) & allowed).astype(BF16)[:, 0:nbp]]
        madd_all = (_dot(jnp.concatenate(sels, axis=0), e_ref[...]) - 1.0) * (-NEG)
        for g in range(G_A):
            q2 = qa_ref[0, g]
            o_cmp = o_cmps[g]
            s_p = _dot(q2, stash_ref[HD_A * g:HD_A * (g + 1), :]) + madd_all[g * nrow:(g + 1) * nrow]
            s_last = s_p[:, past - LANE:past] + ds_ref[g]
            k_new = new_sel[HD_A * g:HD_A * (g + 1)].astype(BF16)
            v_new = new_sel[LANE + HD_A * g:LANE + HD_A * (g + 1)].astype(BF16)
            s_n = _dot(q2, k_new) + dn_ref[g]
            s_all = jnp.concatenate([s_p[:, 0:past - LANE], s_last, s_n], axis=1)
            m = jnp.max(s_all, axis=-1, keepdims=True)
            p = jnp.exp(s_all - m)
            l = jnp.sum(p, axis=-1, keepdims=True)
            pb16 = p.astype(BF16)
            o_sel = (_dot_nt(pb16[:, 0:past], stash_ref[LANE + HD_A * g:LANE + HD_A * (g + 1), :])
                     + _dot_nt(pb16[:, past:past + LANE], v_new)) / l
            kw = win_ref[0, HD_A * g:HD_A * (g + 1), :].astype(BF16)
            vw = win_ref[0, LANE + HD_A * g:LANE + HD_A * (g + 1), :].astype(BF16)
            kw_new = new_win[HD_A * g:HD_A * (g + 1)].astype(BF16)
            vw_new = new_win[LANE + HD_A * g:LANE + HD_A * (g + 1)].astype(BF16)
            s_w = jnp.concatenate([_dot(q2, kw) + bw_ref[g], _dot(q2, kw_new) + bwn_ref[g]], axis=1)
            p_w = _softmax_rows(s_w).astype(BF16)
            o_win = _dot_nt(p_w[:, 0:wb], vw) + _dot_nt(p_w[:, wb:wb + LANE], vw_new)
            gs = gsel_ref[0, g]
            o_ref[0, g] = gs[:, 0:1] * o_cmp + gs[:, 1:2] * o_sel + gs[:, 2:3] * o_win

        win = win_ref[0]
        rolled = pltpu.roll(win, wb - ts, 1)
        new_w = pltpu.roll(new_win, LANE - ts, 1)
        lane = lax.broadcasted_iota(jnp.int32, (win.shape[0], LANE), 1)
        last = jnp.where(lane >= LANE - ts, new_w, rolled[:, wb - LANE:wb])
        wout_ref[0] = jnp.concatenate([rolled[:, 0:wb - LANE], last], axis=1)


def _decode_nsa(page_table, rel_bias, cache_t, qa_s, nsat_s, wint_s, win_t, gsel, cw, mov, emat, ts, past):
    bd, n_pages = page_table.shape
    pps = math.gcd(n_pages, PAGES_PER_STEP)
    n_steps = n_pages // pps
    lperm, wc, pe, wflat = cw
    nrow = R_A * ts
    nch = past // CMP_STRIDE
    wb = win_t.shape[2]
    feat = cache_t.shape[1]
    per_block = LANE // ts

    def full(a):
        return pl.BlockSpec(a.shape, lambda b, s, pt, _n=a.ndim: (0,) * _n, pipeline_mode=pl.Buffered(1))

    def page_spec(k):
        return pl.BlockSpec((1, feat, LANE), lambda b, s, pt, _k=k: (pt[b, s * pps + _k], 0, 0))

    in_specs = ([pl.BlockSpec(memory_space=pltpu.SMEM)]
                + [page_spec(k) for k in range(pps)]
                + [pl.BlockSpec((1, G_A, nrow, HD_A), lambda b, s, pt: (b, 0, 0, 0)),
                   pl.BlockSpec((1, 2 * LANE, LANE), lambda b, s, pt: (0, 1, b // per_block)),
                   pl.BlockSpec((1, 2 * LANE, LANE), lambda b, s, pt: (0, 0, b // per_block)),
                   pl.BlockSpec((1, 2 * LANE, wb), lambda b, s, pt: (b, 0, 0)),
                   pl.BlockSpec((1, G_A, nrow, LANE), lambda b, s, pt: (b, 0, 0, 0)),
                   full(lperm), full(wc), full(pe), full(wflat), full(mov), full(emat)])
    grid_spec = pltpu.PrefetchScalarGridSpec(
        num_scalar_prefetch=1,
        grid=(bd, n_steps),
        in_specs=in_specs,
        out_specs=[pl.BlockSpec((1, G_A, nrow, HD_A), lambda b, s, pt: (b, 0, 0, 0)),
                   pl.BlockSpec((1, 2 * LANE, wb), lambda b, s, pt: (b, 0, 0))],
        scratch_shapes=[pltpu.VMEM((CMP_STRIDE, nch, 2 * LANE), BF16),
                        pltpu.VMEM((2 * LANE, past), BF16),
                        pltpu.VMEM((G_A, nrow, nch), F32),
                        pltpu.VMEM((G_A, nrow, LANE), F32),
                        pltpu.VMEM((G_A, nrow, LANE), F32),
                        pltpu.VMEM((G_A, nrow, wb), F32),
                        pltpu.VMEM((G_A, nrow, LANE), F32)])
    return pl.pallas_call(
        functools.partial(_decode_nsa_kernel, ts=ts, past=past, pps=pps),
        grid_spec=grid_spec,
        out_shape=(jax.ShapeDtypeStruct((bd, G_A, nrow, HD_A), F32),
                   jax.ShapeDtypeStruct((bd, 2 * LANE, wb), F32)),
        compiler_params=pltpu.CompilerParams(vmem_limit_bytes=VMEM_LIMIT,
                                             dimension_semantics=("arbitrary", "arbitrary")),
        name="decode_nsa",
    )(page_table, rel_bias, *([cache_t] * pps), qa_s, nsat_s, wint_s, win_t, gsel,
      lperm, wc, pe, wflat, mov, emat)


def _decode_mla_kernel(pt_ref, *refs, ts, n_pages):
    pages = refs[:n_pages]
    qm_ref, new_ref, o_ref = refs[n_pages:]
    del pt_ref
    nrow = H_B * ts
    q2 = qm_ref[0]
    feat = pages[0].shape[1]
    zpad = jnp.zeros((2 * LANE - feat, LANE), BF16)
    keys, scores = [], []
    for grp in range(n_pages // MLA_PAGE_GROUP):
        kt = jnp.concatenate(
            [jnp.concatenate([pages[grp * MLA_PAGE_GROUP + k][0].astype(BF16), zpad], axis=0)
             for k in range(MLA_PAGE_GROUP)], axis=1)
        keys.append(kt)
        scores.append(_dot(q2, kt))
    new = _own_lanes(new_ref[0], pl.program_id(0), ts)
    kn = jnp.concatenate([new.astype(BF16), zpad], axis=0)
    trow = lax.broadcasted_iota(jnp.int32, (nrow, LANE), 0) % ts
    u = lax.broadcasted_iota(jnp.int32, (nrow, LANE), 1)
    scores.append(_dot(q2, kn) + jnp.where((u <= trow) & (u < ts), 0.0, NEG))
    s_all = jnp.concatenate(scores, axis=1)
    m = jnp.max(s_all, axis=-1, keepdims=True)
    p = jnp.exp(s_all - m)
    l = jnp.sum(p, axis=-1, keepdims=True)
    pb = p.astype(BF16)
    gw = MLA_PAGE_GROUP * LANE
    acc = _dot_nt(pb[:, len(keys) * gw:len(keys) * gw + LANE], kn[0:KV_LORA])
    for grp, kt in enumerate(keys):
        acc = acc + _dot_nt(pb[:, grp * gw:(grp + 1) * gw], kt[0:KV_LORA])
    o_ref[0] = acc / l


def _decode_mla(page_table, cache_t, qm_s, new_mla, ts):
    bd, n_pages = page_table.shape
    nrow = H_B * ts
    feat = cache_t.shape[1]

    def page_spec(k):
        return pl.BlockSpec((1, feat, LANE), lambda b, pt, _k=k: (pt[b, _k], 0, 0))

    grid_spec = pltpu.PrefetchScalarGridSpec(
        num_scalar_prefetch=1,
        grid=(bd,),
        in_specs=([page_spec(k) for k in range(n_pages)]
                  + [pl.BlockSpec((1, nrow, 2 * LANE), lambda b, pt: (b, 0, 0)),
                     pl.BlockSpec((1, feat, LANE), lambda b, pt: (0, 0, b // (LANE // ts)))]),
        out_specs=pl.BlockSpec((1, nrow, KV_LORA), lambda b, pt: (b, 0, 0)))
    return pl.pallas_call(
        functools.partial(_decode_mla_kernel, ts=ts, n_pages=n_pages),
        grid_spec=grid_spec,
        out_shape=jax.ShapeDtypeStruct((bd, nrow, KV_LORA), F32),
        compiler_params=pltpu.CompilerParams(vmem_limit_bytes=VMEM_LIMIT,
                                             dimension_semantics=("arbitrary",)),
        name="decode_mla",
    )(page_table, *([cache_t] * n_pages), qm_s, new_mla)


def _rope_tables(pos):
    half = ROPE_DIM // 2
    inv = ROPE_BASE ** (-jnp.arange(half, dtype=F32) / half)
    ang = pos.astype(F32)[:, None] * inv[None, :]
    cos, sin = jnp.cos(ang), jnp.sin(ang)
    z = jnp.zeros((pos.shape[0], LANE - ROPE_DIM), F32)
    cos_tm = jnp.tile(jnp.concatenate([cos, cos, z], axis=1), (1, H_B))
    sin_tm = jnp.tile(jnp.concatenate([sin, sin, z], axis=1), (1, H_B))
    return cos_tm, sin_tm, cos.T, sin.T


def _overlap_t(n_cmp_rows, n_sel):
    cs = jnp.arange(n_cmp_rows)[None, :] * CMP_STRIDE
    ss = jnp.arange(n_sel)[:, None] * SEL_BLOCK
    ov = jnp.minimum(cs + CMP_BLOCK, ss + SEL_BLOCK) - jnp.maximum(cs, ss)
    ov = jnp.maximum(ov, 0).astype(F32) / CMP_BLOCK
    return jnp.where(jnp.arange(n_cmp_rows)[None, :] < n_cmp_rows - 1, ov, 0.0)


def _compress_weights(w_k, w_v, pe_k, pe_v):
    eye = jnp.eye(G_A, dtype=F32)

    def pair_blocks(w):
        w6 = w.reshape(2, CMP_STRIDE // 2, 2, HD_A, HD_A)
        blk = jnp.einsum('hpjde,gk->pjgdhke', w6, eye)
        return blk.reshape(CMP_STRIDE // 2, 2 * LANE, 2 * LANE)

    wc = jnp.stack([pair_blocks(w_k), pair_blocks(w_v)]).astype(BF16)
    pe = jnp.stack([jnp.tile(pe_k.reshape(1, -1), (8, 1)), jnp.tile(pe_v.reshape(1, -1), (8, 1))]).astype(BF16)
    wflat = jnp.stack([jnp.tile(w_k.reshape(-1, HD_A), (1, G_A)),
                       jnp.tile(w_v.reshape(-1, HD_A), (1, G_A))]).astype(BF16)
    r = jnp.arange(256)
    lperm = (jnp.arange(256)[None, :] == ((r % 16) * 16 + r // 16)[:, None]).astype(BF16)
    return lperm, wc, pe, wflat


def kernel(x_prompt, x_sample, cache_nsa, cache_mla, state_win, page_table, c_prompt, c_sample,
           w_ada, b_ada, g_norm, w_in, w_cmp_k, w_cmp_v, pe_cmp_k, pe_cmp_v, rel_bias,
           g_q_lat, w_uq, g_kv_lat, w_uk, w_uv, w_out, g_final):
    assert w_in.shape[0] == 1, "one layer"
    nb, t, d = x_prompt.shape
    bd, ts, _ = x_sample.shape
    n_pages = page_table.shape[1]
    page = cache_nsa.shape[2]
    past = n_pages * page
    assert page == LANE and (past + ts) // CMP_STRIDE == past // CMP_STRIDE and ts <= SEL_BLOCK
    assert (R_A * ts // 2) % ts == 0 and t % TQM == 0 and LANE % ts == 0 and (bd * ts) % LANE == 0
    assert t % KT == 0 and (bd * ts) % TM == 0 and n_pages % MLA_PAGE_GROUP == 0 and n_pages % 2 == 0
    wb = state_win.shape[2]

    wi = w_in[0]
    o1 = W_A
    o2 = o1 + 6 * G_A * HD_A
    o3 = o2 + 3 * H_A
    o4 = o3 + W_A
    o5 = o4 + Q_LORA
    o6 = o5 + KV_LORA
    o7 = o6 + ROPE_DIM
    wg = jnp.pad(wi[:, o2:o3], ((0, 0), (0, LANE - 3 * H_A)))
    wt = jnp.concatenate([wi[:, 0:o1] * (HD_A ** -0.5), wi[:, o3:o4], wi[:, o7:], wi[:, o4:o5], wg], axis=1).astype(BF16)
    wf = jnp.concatenate([wi[:, o1:o2], wi[:, o5:o6], wi[:, o6:o7]], axis=1).T.astype(BF16)
    wq = w_uq[0]
    wqn = wq[:, :, :NOPE].reshape(Q_LORA, H_B * NOPE).astype(BF16)
    wr = wq[:, :, NOPE:]
    half = ROPE_DIM // 2
    wsw = jnp.concatenate([-wr[..., half:], wr[..., :half]], axis=-1)
    padr = ((0, 0), (0, 0), (0, LANE - ROPE_DIM))
    wqr = jnp.pad(wr, padr).reshape(Q_LORA, H_B * LANE).astype(BF16)
    wqs = jnp.pad(wsw, padr).reshape(Q_LORA, H_B * LANE).astype(BF16)
    wuk = jnp.transpose(w_uk[0], (1, 2, 0)).astype(BF16)
    wuv = jnp.transpose(w_uv[0], (1, 0, 2)).astype(BF16)
    wts = (g_norm[0][None], wt, wf, g_q_lat[0][None], wqn, wqr, wqs, wuk, g_kv_lat[0][:, None])
    cw = _compress_weights(w_cmp_k[0], w_cmp_v[0], pe_cmp_k[0], pe_cmp_v[0])
    wout = w_out[0].astype(BF16)
    gfin = g_final[None]

    n_c = nb + bd
    n_cp = -(-n_c // 8) * 8
    c_all = jnp.pad(jnp.concatenate([c_prompt, c_sample], axis=0), ((0, n_cp - n_c), (0, 0)))
    mod = _modulation(c_all, w_ada[0].astype(BF16), b_ada)
    shift, scale, gate = mod[:, 0:d], mod[:, d:2 * d], mod[:, 2 * d:3 * d]

    def prompt_mod(a):
        return a[0:nb][:, None, :]

    def sample_mod(a):
        return jnp.repeat(a[nb:nb + bd], ts, axis=0).reshape(bd * ts // TM, TM, d)

    pos_p = jnp.arange(t, dtype=jnp.int32)
    xp = x_prompt.reshape(nb * t, d)
    (qa, gates, sza, szb, qm, nsat, wint, kvb, mlat, mlab) = _project(
        xp, prompt_mod(shift), prompt_mod(scale), False, nb, t, wts, *_rope_tables(pos_p))
    cmpk = _prompt_compress(kvb, cw)
    nch_p = t // CMP_STRIDE
    ns_p = t // SEL_BLOCK
    movt = _overlap_t(nch_p, ns_p).astype(BF16)
    emat = jnp.where(jnp.arange(t)[None, :] // SEL_BLOCK == jnp.arange(ns_p)[:, None], NEG, 0.0).astype(BF16)
    oa = _prompt_nsa(rel_bias, qa, kvb, cmpk, gates, movt, emat)
    ol = _prompt_mla(qm, mlab)
    y_prompt = _merge(xp, prompt_mod(gate), False, t, oa, ol, sza, szb, wuv, wout, gfin).reshape(nb, t, d)

    n_s = bd * ts
    pos_s = past + jnp.arange(ts, dtype=jnp.int32)
    cos_tm, sin_tm, cos_t, sin_t = _rope_tables(pos_s)
    xs = x_sample.reshape(n_s, d)
    (qa_s, gates_s, sza_s, szb_s, qm_s, nsat_s, wint_s, _, mlat_s, _) = _project(
        xs, sample_mod(shift), sample_mod(scale), True, 1, n_s, wts,
        jnp.tile(cos_tm, (bd, 1)), jnp.tile(sin_tm, (bd, 1)), jnp.tile(cos_t, (1, bd)), jnp.tile(sin_t, (1, bd)))
    nrow = R_A * ts
    qa_g = qa_s.reshape(bd, ts, G_A, R_A, HD_A).transpose(0, 2, 3, 1, 4).reshape(bd, G_A, nrow, HD_A)
    gsel = gates_s[:, 0:3 * H_A].reshape(bd, ts, G_A, R_A, 3).transpose(0, 2, 3, 1, 4).reshape(bd, G_A, nrow, 3)
    gsel = jnp.pad(gsel, ((0, 0), (0, 0), (0, 0), (0, LANE - 3)))
    qm_h = qm_s.reshape(bd, ts, H_B, 2 * LANE).transpose(0, 2, 1, 3).reshape(bd, H_B * ts, 2 * LANE)
    n_phys = cache_nsa.shape[1]
    cache_nsa_t = jnp.transpose(cache_nsa[0], (0, 2, 3, 4, 1)).reshape(n_phys, 4 * G_A * HD_A, page)
    cache_mla_t = jnp.transpose(cache_mla[0], (0, 2, 1))
    win_t = jnp.transpose(state_win[0], (0, 2, 3, 4, 1)).reshape(bd, 2 * G_A * HD_A, wb)
    nch_s = past // CMP_STRIDE
    nbp = past // SEL_BLOCK
    nsp = -(-(nbp + 1) // LANE) * LANE
    mov_s = jnp.pad(_overlap_t(nch_s, nbp + 1).T, ((0, 0), (0, nsp - nbp - 1))).astype(BF16)
    emat_s = (jnp.arange(past)[None, :] // SEL_BLOCK == jnp.arange(nbp)[:, None]).astype(BF16)
    oa_g, win_new = _decode_nsa(page_table, rel_bias, cache_nsa_t, qa_g, nsat_s, wint_s, win_t, gsel, cw,
                                mov_s, emat_s, ts, past)
    ol_h = _decode_mla(page_table, cache_mla_t, qm_h, mlat_s, ts)
    oa_s = oa_g.reshape(bd, G_A, R_A, ts, HD_A).transpose(0, 3, 1, 2, 4).reshape(n_s, W_A)
    ol_s = ol_h.reshape(bd, H_B, ts, KV_LORA).transpose(0, 2, 1, 3).reshape(n_s, H_B * KV_LORA)
    y_sample = _merge(xs, sample_mod(gate), True, n_s, oa_s, ol_s, sza_s, szb_s, wuv, wout, gfin).reshape(bd, ts, d)

    new_nsa_prompt = nsat.reshape(nb, 4, G_A, HD_A, t).transpose(0, 4, 1, 2, 3)[None]
    new_nsa_sample = nsat_s[0].reshape(4, G_A, HD_A, bd, ts).transpose(3, 4, 0, 1, 2)[None]
    new_mla_prompt = mlat.transpose(0, 2, 1)[None]
    new_mla_sample = mlat_s[0].reshape(KV_LORA + ROPE_DIM, bd, ts).transpose(1, 2, 0)[None]
    wlen = min(WINDOW, t)
    new_win_prompt = wint[:, :, t - wlen:].reshape(nb, 2, G_A, HD_A, wlen).transpose(0, 4, 1, 2, 3)[None]
    new_win_sample = win_new.reshape(bd, 2, G_A, HD_A, wb).transpose(0, 4, 1, 2, 3)[None]
    return (y_prompt, y_sample, new_nsa_prompt, new_nsa_sample, new_mla_prompt, new_mla_sample,
            new_win_prompt, new_win_sample)
```

```python
import functools
import math

import jax
import jax.numpy as jnp
from jax import lax
from jax.experimental import pallas as pl
from jax.experimental.pallas import tpu as pltpu

F32 = jnp.float32
BF16 = jnp.bfloat16

H_A, G_A, R_A, HD_A = 8, 2, 4, 64
W_A = H_A * HD_A
CMP_BLOCK, CMP_STRIDE = 32, 16
SEL_BLOCK, N_SEL = 64, 16
WINDOW = 512
FORCE_SCORE = 1e4
H_B, NOPE, ROPE_DIM, V_HEAD, Q_LORA, KV_LORA = 4, 128, 64, 128, 256, 128
W_B = H_B * V_HEAD
ROPE_BASE = 10000.0
N_BUCKETS, MAX_DIST = 32, 128
EPS = 1e-6
NEG = -1e30

TQ = 128
TQM = 256
KT = 1024
TM = 512
LANE = 128
PAGES_PER_STEP = 64
MLA_PAGE_GROUP = 16
VMEM_LIMIT = 56 * 1024 * 1024


def _row_tile(rows):
    return TM if rows % TM == 0 else TM // 2


def _dot(a, b):
    return jnp.dot(a, b, preferred_element_type=F32)


def _dot_nt(a, b):
    return lax.dot_general(a, b, (((1,), (1,)), ((), ())), preferred_element_type=F32)


def _rel_bucket(dist):
    n = jnp.maximum(dist, 0)
    max_exact = N_BUCKETS // 2
    span = N_BUCKETS - max_exact
    large = jnp.full(n.shape, max_exact, jnp.int32)
    for k in range(1, span):
        thr = math.ceil(max_exact * (MAX_DIST / max_exact) ** (k / span))
        large = large + (n >= thr).astype(jnp.int32)
    return jnp.where(n < max_exact, n, large)


def _bias_lookup(bucket, tab_ref, h):
    acc = jnp.zeros(bucket.shape, F32)
    for k in range(N_BUCKETS):
        acc = jnp.where(bucket == k, tab_ref[k, h], acc)
    return acc


def _softmax_rows(s):
    m = jnp.max(s, axis=-1, keepdims=True)
    p = jnp.exp(s - m)
    l = jnp.sum(p, axis=-1, keepdims=True)
    return jnp.where(s > 0.5 * NEG, p / l, 0.0)


def _mod_kernel(c_ref, w_ref, b_ref, o_ref):
    o_ref[...] = _dot(c_ref[...].astype(BF16), w_ref[...]) + b_ref[...]


def _modulation(c, w_ada_b, b_ada):
    n, d = c.shape
    n3 = w_ada_b.shape[1]
    tn = 1024
    return pl.pallas_call(
        _mod_kernel,
        grid=(n3 // tn,),
        in_specs=[pl.BlockSpec((n, d), lambda j: (0, 0)),
                  pl.BlockSpec((d, tn), lambda j: (0, j)),
                  pl.BlockSpec((1, tn), lambda j: (0, j))],
        out_specs=pl.BlockSpec((n, tn), lambda j: (0, j)),
        out_shape=jax.ShapeDtypeStruct((n, n3), F32),
        name="adaln_mod",
    )(c, w_ada_b, b_ada)


def _proj_kernel(x_ref, shift_ref, scale_ref, gn_ref, wt_ref, wf_ref, gq_ref, wqn_ref, wqr_ref,
                 wqs_ref, wuk_ref, gkv_ref, cos_ref, sin_ref, cost_ref, sint_ref,
                 qa_ref, gates_ref, sza_ref, szb_ref, qm_ref, nsat_ref, wint_ref, kvb_ref,
                 mlat_ref, mlab_ref, *, mla_scale):
    x = x_ref[...]
    y = x * lax.rsqrt(jnp.mean(x * x, axis=-1, keepdims=True) + EPS) * gn_ref[...]
    h = y * (1.0 + scale_ref[0]) + shift_ref[0]
    hb = h.astype(BF16)

    ut = _dot(hb, wt_ref[...])
    qa_ref[...] = ut[:, 0:W_A].astype(BF16)
    za = ut[:, W_A:2 * W_A]
    zb = ut[:, 2 * W_A:3 * W_A]
    sza_ref[...] = za * jax.nn.sigmoid(za)
    szb_ref[...] = zb * jax.nn.sigmoid(zb)
    cq = ut[:, 3 * W_A:3 * W_A + Q_LORA]
    gates_ref[...] = jax.nn.sigmoid(ut[:, 3 * W_A + Q_LORA:3 * W_A + Q_LORA + LANE])

    cqn = (cq * lax.rsqrt(jnp.mean(cq * cq, axis=-1, keepdims=True) + EPS) * gq_ref[...]).astype(BF16)
    qn = _dot(cqn, wqn_ref[...])
    qr = _dot(cqn, wqr_ref[...])
    qs = _dot(cqn, wqs_ref[...])
    q_rope = qr * cos_ref[...] + qs * sin_ref[...]
    for hh in range(H_B):
        q_lat = _dot(qn[:, hh * NOPE:(hh + 1) * NOPE].astype(BF16), wuk_ref[hh])
        qm_ref[:, 2 * LANE * hh:2 * LANE * hh + LANE] = (q_lat * mla_scale).astype(BF16)
        qm_ref[:, 2 * LANE * hh + LANE:2 * LANE * (hh + 1)] = (
            q_rope[:, hh * LANE:(hh + 1) * LANE] * mla_scale).astype(BF16)

    uf = _dot_nt(wf_ref[...], hb)
    n_sel_rows = 4 * G_A * HD_A
    n_kv_rows = 6 * G_A * HD_A
    nsat_ref[0] = uf[0:n_sel_rows]
    wint_ref[0] = uf[n_sel_rows:n_kv_rows]
    kvb_ref[0] = uf[0:n_kv_rows].astype(BF16)
    c = uf[n_kv_rows:n_kv_rows + KV_LORA]
    cn = c * lax.rsqrt(jnp.mean(c * c, axis=0, keepdims=True) + EPS) * gkv_ref[...]
    kr = uf[n_kv_rows + KV_LORA:n_kv_rows + KV_LORA + ROPE_DIM]
    half = ROPE_DIM // 2
    x1, x2 = kr[0:half], kr[half:ROPE_DIM]
    ct, st = cost_ref[...], sint_ref[...]
    mla = jnp.concatenate([cn, x1 * ct - x2 * st, x1 * st + x2 * ct], axis=0)
    mlat_ref[0] = mla
    n_pad = 2 * LANE - KV_LORA - ROPE_DIM
    pad = jnp.where(lax.broadcasted_iota(jnp.int32, (n_pad, mla.shape[1]), 0) == 0, 1.0, 0.0)
    mlab_ref[0] = jnp.concatenate([mla, pad], axis=0).astype(BF16)


def _project(xf, shift3, scale3, per_row_mod, nb, tb, wts, cos_tm, sin_tm, cos_t, sin_t):
    n, d = xf.shape
    tm = _row_tile(tb)
    tiles_b = tb // tm
    ntab = cos_tm.shape[0] // tm
    (gn, wt, wf, gq, wqn, wqr, wqs, wuk, gkv) = wts
    if per_row_mod:
        mod_spec = pl.BlockSpec((1, tm, d), lambda r: (r, 0, 0))
    else:
        mod_spec = pl.BlockSpec((1, 1, d), lambda r: (r // tiles_b, 0, 0))

    def full(a):
        return pl.BlockSpec(a.shape, lambda r, _n=a.ndim: (0,) * _n)

    def fm(rows):
        return pl.BlockSpec((1, rows, tm), lambda r: (r // tiles_b, 0, r % tiles_b))

    def tmaj(cols):
        return pl.BlockSpec((tm, cols), lambda r: (r, 0))

    n_sel_rows = 4 * G_A * HD_A
    out_shape = (
        jax.ShapeDtypeStruct((n, W_A), BF16),
        jax.ShapeDtypeStruct((n, LANE), F32),
        jax.ShapeDtypeStruct((n, W_A), F32),
        jax.ShapeDtypeStruct((n, W_B), F32),
        jax.ShapeDtypeStruct((n, H_B * 2 * LANE), BF16),
        jax.ShapeDtypeStruct((nb, n_sel_rows, tb), F32),
        jax.ShapeDtypeStruct((nb, 2 * G_A * HD_A, tb), F32),
        jax.ShapeDtypeStruct((nb, 6 * G_A * HD_A, tb), BF16),
        jax.ShapeDtypeStruct((nb, KV_LORA + ROPE_DIM, tb), F32),
        jax.ShapeDtypeStruct((nb, 2 * LANE, tb), BF16),
    )
    out_specs = (tmaj(W_A), tmaj(LANE), tmaj(W_A), tmaj(W_B), tmaj(H_B * 2 * LANE),
                 fm(n_sel_rows), fm(2 * G_A * HD_A), fm(6 * G_A * HD_A), fm(KV_LORA + ROPE_DIM), fm(2 * LANE))
    in_specs = [tmaj(d), mod_spec, mod_spec, full(gn), full(wt), full(wf), full(gq), full(wqn),
                full(wqr), full(wqs), full(wuk), full(gkv),
                pl.BlockSpec((tm, H_B * LANE), lambda r: (r % ntab, 0)),
                pl.BlockSpec((tm, H_B * LANE), lambda r: (r % ntab, 0)),
                pl.BlockSpec((ROPE_DIM // 2, tm), lambda r: (0, r % ntab)),
                pl.BlockSpec((ROPE_DIM // 2, tm), lambda r: (0, r % ntab))]
    return pl.pallas_call(
        functools.partial(_proj_kernel, mla_scale=(NOPE + ROPE_DIM) ** -0.5),
        grid=(n // tm,),
        in_specs=in_specs,
        out_specs=out_specs,
        out_shape=out_shape,
        compiler_params=pltpu.CompilerParams(vmem_limit_bytes=VMEM_LIMIT),
        name="in_proj",
    )(xf, shift3, scale3, gn, wt, wf, gq, wqn, wqr, wqs, wuk, gkv, cos_tm, sin_tm, cos_t, sin_t)


def _chunk_rows(lperm_ref, feats_tok):
    return _dot_nt(lperm_ref[...], feats_tok).astype(BF16)


def _compress_chunks(a_ref, wc_ref, pe_ref, wflat_ref, nch):
    outs = []
    for kind in range(2):
        acc = jnp.zeros((nch, 2 * LANE), F32)
        for p in range(CMP_STRIDE // 2):
            lhs = jnp.concatenate([a_ref[2 * p, :, kind * LANE:(kind + 1) * LANE],
                                   a_ref[2 * p + 1, :, kind * LANE:(kind + 1) * LANE]], axis=1)
            acc = acc + _dot(lhs, wc_ref[kind, p])
        first, second = acc[:, 0:LANE], acc[:, LANE:2 * LANE]
        cb = _dot(pe_ref[kind], wflat_ref[kind])[0:1]
        outs.append(first + pltpu.roll(second, nch - 1, 0) + cb)
    return jnp.concatenate(outs, axis=1)


def _prompt_compress_kernel(kv_ref, lperm_ref, wc_ref, pe_ref, wflat_ref, o_ref, a_ref):
    t = kv_ref.shape[2]
    nch = t // CMP_STRIDE
    for tix in range(t // 256):
        tb = _chunk_rows(lperm_ref, kv_ref[0, :, tix * 256:(tix + 1) * 256])
        for j in range(CMP_STRIDE):
            a_ref[j, tix * 16:(tix + 1) * 16, :] = tb[16 * j:16 * (j + 1)]
    o_ref[0] = _compress_chunks(a_ref, wc_ref, pe_ref, wflat_ref, nch).astype(BF16)


def _prompt_compress(kvb, cw):
    nb, _, t = kvb.shape
    nch = t // CMP_STRIDE
    lperm, wc, pe, wflat = cw

    def full(a):
        return pl.BlockSpec(a.shape, lambda b, _n=a.ndim: (0,) * _n)

    return pl.pallas_call(
        _prompt_compress_kernel,
        grid=(nb,),
        in_specs=[pl.BlockSpec((1, 2 * LANE, t), lambda b: (b, 0, 0)), full(lperm), full(wc), full(pe), full(wflat)],
        out_specs=pl.BlockSpec((1, nch, 2 * LANE), lambda b: (b, 0, 0)),
        out_shape=jax.ShapeDtypeStruct((nb, nch, 2 * LANE), BF16),
        scratch_shapes=[pltpu.VMEM((CMP_STRIDE, nch, 2 * LANE), BF16)],
        compiler_params=pltpu.CompilerParams(vmem_limit_bytes=VMEM_LIMIT),
        name="prompt_compress",
    )(kvb, lperm, wc, pe, wflat)


def _rank(score, n_valid, axis):
    size = 8 if axis == 0 else LANE
    ntile = score.shape[axis] // size

    def tile(a, k):
        return a[k * size:(k + 1) * size, :] if axis == 0 else a[:, k * size:(k + 1) * size]

    tiles = [tile(score, k) for k in range(ntile)]
    ranks = [jnp.zeros(tiles[0].shape, F32) for _ in range(ntile)]
    pos = lax.broadcasted_iota(jnp.int32, tiles[0].shape, axis)
    for j2 in range(n_valid):
        kt, off = divmod(j2, size)
        cand = tiles[kt][off:off + 1, :] if axis == 0 else tiles[kt][:, off:off + 1]
        for k in range(ntile):
            if k > kt:
                before = cand >= tiles[k]
            elif k < kt:
                before = cand > tiles[k]
            else:
                before = (cand > tiles[k]) | ((cand == tiles[k]) & (pos > off))
            ranks[k] = ranks[k] + jnp.where(before, 1.0, 0.0)
    return jnp.concatenate(ranks, axis=axis)


def _prompt_nsa_kernel(tab_ref, qa_ref, sel_ref, win_ref, cmp_ref, gates_ref, movt_ref, eneg_ref,
                       o_ref, bwin_ref, dsel_ref, gcmp_ref, dcmp_ref, sc_ref):
    i = pl.program_id(0)
    b = pl.program_id(1)
    t = sel_ref.shape[2]
    nch = cmp_ref.shape[1]
    ns = movt_ref.shape[0]
    n_win_tiles = WINDOW // TQ + 1
    chunks_per_tile = TQ // CMP_STRIDE
    m_off = chunks_per_tile * (t // TQ - 1)

    @pl.when((i == 0) & (b == 0))
    def _():
        tl = lax.broadcasted_iota(jnp.int32, (TQ, n_win_tiles * TQ), 0)
        kl = lax.broadcasted_iota(jnp.int32, (TQ, n_win_tiles * TQ), 1)
        dist = tl + WINDOW - kl
        bucket = _rel_bucket(dist)
        ok = (dist >= 0) & (dist < WINDOW)
        tl2 = lax.broadcasted_iota(jnp.int32, (TQ, 2 * TQ), 0)
        kl2 = lax.broadcasted_iota(jnp.int32, (TQ, 2 * TQ), 1)
        dist2 = tl2 + TQ - kl2
        bucket2 = _rel_bucket(dist2)
        for h in range(H_A):
            bwin_ref[h] = jnp.where(ok, _bias_lookup(bucket, tab_ref, h), NEG)
            dsel_ref[h] = jnp.where(dist2 >= 0,
                                    _bias_lookup(bucket2, tab_ref, h) - tab_ref[N_BUCKETS - 1, h], NEG)
        tl3 = lax.broadcasted_iota(jnp.int32, (TQ, 2 * nch), 0)
        m3 = lax.broadcasted_iota(jnp.int32, (TQ, 2 * nch), 1)
        dist3 = tl3 - (m3 - m_off) * CMP_STRIDE - (CMP_BLOCK - 1)
        bucket3 = _rel_bucket(dist3)
        for h in range(H_A):
            gcmp_ref[h] = jnp.where(dist3 >= 0,
                                    _bias_lookup(bucket3, tab_ref, h) - tab_ref[N_BUCKETS - 1, h], NEG)

    @pl.when(b == 0)
    def _():
        shift = (2 * nch - (m_off - chunks_per_tile * i)) % (2 * nch)
        for h in range(H_A):
            dcmp_ref[h] = pltpu.roll(gcmp_ref[h], shift, 1)[:, 0:nch]

    i_prev = jnp.maximum(i - 1, 0)
    n_main = (i_prev * TQ + KT - 1) // KT
    c_a = pl.multiple_of(i_prev * TQ, TQ)
    c_b = pl.multiple_of(i * TQ, TQ)
    blocks_per_tile = TQ // SEL_BLOCK
    ones_rows = jnp.ones((16, KT), BF16)
    gates = gates_ref[...]

    def k_ext(g, c0, width):
        return jnp.concatenate([sel_ref[0, HD_A * g:HD_A * (g + 1), pl.ds(c0, width)],
                                eneg_ref[:, pl.ds(c0, width)]], axis=0)

    def v_ext(g, c0, width):
        return jnp.concatenate([sel_ref[0, LANE + HD_A * g:LANE + HD_A * (g + 1), pl.ds(c0, width)],
                                ones_rows[:, 0:width]], axis=0)

    q2s, o_cmps, lhs_mains, lhs_tails = [], [], [], []
    for g in range(G_A):
        q2 = jnp.concatenate([qa_ref[:, HD_A * (R_A * g + r):HD_A * (R_A * g + r + 1)] for r in range(R_A)],
                             axis=0)
        kc = cmp_ref[0, :, HD_A * g:HD_A * (g + 1)]
        vc = cmp_ref[0, :, LANE + HD_A * g:LANE + HD_A * (g + 1)]
        s = _dot_nt(q2, kc).reshape(R_A, TQ, nch) + dcmp_ref[R_A * g:R_A * (g + 1)]
        p_c = _softmax_rows(s.reshape(R_A * TQ, nch))
        p_cb = p_c.astype(BF16)
        o_cmp = _dot(p_cb, vc)
        imp_t = jnp.zeros((ns, TQ), F32)
        for r in range(R_A):
            imp_t = imp_t + _dot_nt(movt_ref[...], p_cb[r * TQ:(r + 1) * TQ])
        jj = lax.broadcasted_iota(jnp.int32, (ns, TQ), 0)
        cc = (i * TQ + lax.broadcasted_iota(jnp.int32, (ns, TQ), 1)) // SEL_BLOCK
        forced = (jj == 0) | (jj == cc) | (jj == cc - 1)
        score = jnp.where(forced, FORCE_SCORE, jnp.where(jj <= cc, imp_t, -jnp.inf))
        rank = _rank(score, ns, 0)
        unsel_t = jnp.where((rank < min(N_SEL, ns)) & (jj <= cc), 0.0, 1.0)
        unsel_main = jnp.transpose(jnp.where(jj < blocks_per_tile * (i - 1), unsel_t, 1.0)).astype(BF16)
        unsel_tail = jnp.transpose(unsel_t).astype(BF16)

        q2s.append(q2)
        o_cmps.append(o_cmp)
        lhs_mains.append(jnp.concatenate([q2, jnp.concatenate([unsel_main] * R_A, axis=0)], axis=1))
        lhs_tails.append(jnp.concatenate([q2, jnp.concatenate([unsel_tail] * R_A, axis=0)], axis=1))

    def score_tile(kt, ms):
        c0 = pl.multiple_of(kt * KT, KT)
        out = []
        for g in range(G_A):
            s = _dot(lhs_mains[g], k_ext(g, c0, KT))
            sc_ref[g, :, pl.ds(c0, KT)] = s
            out.append(jnp.maximum(ms[g], jnp.max(s, axis=-1, keepdims=True)))
        return tuple(out)

    ms = lax.fori_loop(0, n_main, score_tile, tuple(jnp.full((R_A * TQ, 1), NEG, F32) for _ in range(G_A)))
    tails = []
    for g in range(G_A):
        d_here = dsel_ref[R_A * g:R_A * (g + 1)].reshape(R_A * TQ, 2 * TQ)
        s_a = _dot(lhs_tails[g], k_ext(g, c_a, TQ)) + d_here[:, 0:TQ] + jnp.where(i == 0, NEG, 0.0)
        s_b = _dot(lhs_tails[g], k_ext(g, c_b, TQ)) + d_here[:, TQ:2 * TQ]
        m = jnp.maximum(ms[g], jnp.maximum(jnp.max(s_a, axis=-1, keepdims=True),
                                           jnp.max(s_b, axis=-1, keepdims=True)))
        tails.append((s_a, s_b, m))

    def pv_tile(kt, accs):
        c0 = pl.multiple_of(kt * KT, KT)
        out = []
        for g in range(G_A):
            p = jnp.exp(sc_ref[g, :, pl.ds(c0, KT)] - tails[g][2]).astype(BF16)
            out.append(accs[g] + _dot_nt(p, v_ext(g, c0, KT)))
        return tuple(out)

    accs = lax.fori_loop(0, n_main, pv_tile,
                         tuple(jnp.zeros((R_A * TQ, HD_A + 16), F32) for _ in range(G_A)))

    for g in range(G_A):
        q2, o_cmp = q2s[g], o_cmps[g]
        s_a, s_b, m = tails[g]
        acc = (accs[g] + _dot_nt(jnp.exp(s_a - m).astype(BF16), v_ext(g, c_a, TQ))
               + _dot_nt(jnp.exp(s_b - m).astype(BF16), v_ext(g, c_b, TQ)))

        s_w = []
        for w in range(n_win_tiles):
            kt = i - (n_win_tiles - 1) + w
            c0 = pl.multiple_of(jnp.maximum(kt, 0) * TQ, TQ)
            kw = win_ref[0, HD_A * g:HD_A * (g + 1), pl.ds(c0, TQ)]
            bw = bwin_ref[R_A * g:R_A * (g + 1), :, w * TQ:(w + 1) * TQ].reshape(R_A * TQ, TQ)
            s_w.append(_dot(q2, kw) + bw + jnp.where(kt < 0, NEG, 0.0))
        s_w = jnp.concatenate(s_w, axis=1)
        p_w = jnp.exp(s_w - jnp.max(s_w, axis=-1, keepdims=True)).astype(BF16)
        acc_w = jnp.zeros((R_A * TQ, HD_A + 16), F32)
        for w in range(n_win_tiles):
            kt = i - (n_win_tiles - 1) + w
            c0 = pl.multiple_of(jnp.maximum(kt, 0) * TQ, TQ)
            vw = jnp.concatenate([win_ref[0, LANE + HD_A * g:LANE + HD_A * (g + 1), pl.ds(c0, TQ)],
                                  ones_rows[:, 0:TQ]], axis=0)
            acc_w = acc_w + _dot_nt(p_w[:, w * TQ:(w + 1) * TQ], vw)

        for r in range(R_A):
            h = R_A * g + r
            rows = slice(r * TQ, (r + 1) * TQ)
            c_sel = gates[:, 3 * h + 1:3 * h + 2] / acc[rows, HD_A:HD_A + 1]
            c_win = gates[:, 3 * h + 2:3 * h + 3] / acc_w[rows, HD_A:HD_A + 1]
            o_ref[:, HD_A * h:HD_A * (h + 1)] = (gates[:, 3 * h:3 * h + 1] * o_cmp[rows]
                                                  + c_sel * acc[rows, 0:HD_A] + c_win * acc_w[rows, 0:HD_A])


def _prompt_nsa(rel_bias, qa, kvb, cmpk, gates, movt, emat):
    nb, _, t = kvb.shape
    ni = t // TQ
    nch = cmpk.shape[1]
    n_win_tiles = WINDOW // TQ + 1

    def full(a):
        return pl.BlockSpec(a.shape, lambda i, b, _n=a.ndim: (0,) * _n)

    return pl.pallas_call(
        _prompt_nsa_kernel,
        grid=(ni, nb),
        in_specs=[pl.BlockSpec(memory_space=pltpu.SMEM),
                  pl.BlockSpec((TQ, W_A), lambda i, b: (b * ni + i, 0)),
                  pl.BlockSpec((1, 2 * LANE, t), lambda i, b: (b, 1, 0)),
                  pl.BlockSpec((1, 2 * LANE, t), lambda i, b: (b, 2, 0)),
                  pl.BlockSpec((1, nch, 2 * LANE), lambda i, b: (b, 0, 0)),
                  pl.BlockSpec((TQ, LANE), lambda i, b: (b * ni + i, 0)),
                  full(movt), full(emat)],
        out_specs=pl.BlockSpec((TQ, W_A), lambda i, b: (b * ni + i, 0)),
        out_shape=jax.ShapeDtypeStruct((nb * t, W_A), F32),
        scratch_shapes=[pltpu.VMEM((H_A, TQ, n_win_tiles * TQ), F32),
                        pltpu.VMEM((H_A, TQ, 2 * TQ), F32),
                        pltpu.VMEM((H_A, TQ, 2 * nch), F32),
                        pltpu.VMEM((H_A, TQ, nch), F32),
                        pltpu.VMEM((G_A, R_A * TQ, t), F32)],
        compiler_params=pltpu.CompilerParams(vmem_limit_bytes=VMEM_LIMIT,
                                             dimension_semantics=("arbitrary", "arbitrary")),
        name="prompt_nsa",
    )(rel_bias, qa, kvb, kvb, cmpk, gates, movt, emat)


def _prompt_mla_kernel(qm_ref, mla_ref, o_ref, sc_ref, acc_ref):
    i = pl.program_id(1)
    nrow = H_B * TQM
    n_full = (i * TQM) // KT
    q2 = jnp.concatenate([qm_ref[:, 2 * LANE * h:2 * LANE * (h + 1)] for h in range(H_B)], axis=0)

    def score_tile(kt, m):
        c0 = pl.multiple_of(kt * KT, KT)
        s = _dot(q2, mla_ref[0, :, pl.ds(c0, KT)])
        sc_ref[:, pl.ds(c0, KT)] = s
        return jnp.maximum(m, jnp.max(s, axis=-1, keepdims=True))

    m = lax.fori_loop(0, n_full, score_tile, jnp.full((nrow, 1), NEG, F32))
    c_d = pl.multiple_of(n_full * KT, KT)
    qpos = i * TQM + lax.broadcasted_iota(jnp.int32, (H_B, TQM, KT), 1).reshape(nrow, KT)
    kpos = c_d + lax.broadcasted_iota(jnp.int32, (nrow, KT), 1)
    s_d = jnp.where(kpos <= qpos, _dot(q2, mla_ref[0, :, pl.ds(c_d, KT)]), NEG)
    m = jnp.maximum(m, jnp.max(s_d, axis=-1, keepdims=True))
    acc_ref[...] = _dot_nt(jnp.exp(s_d - m).astype(BF16), mla_ref[0, :, pl.ds(c_d, KT)])

    def pv_tile(kt, carry):
        c0 = pl.multiple_of(kt * KT, KT)
        p = jnp.exp(sc_ref[:, pl.ds(c0, KT)] - m).astype(BF16)
        acc_ref[...] += _dot_nt(p, mla_ref[0, :, pl.ds(c0, KT)])
        return carry

    lax.fori_loop(0, n_full, pv_tile, 0)
    acc = acc_ref[...]
    o = acc[:, 0:KV_LORA] / acc[:, KV_LORA + ROPE_DIM:KV_LORA + ROPE_DIM + 1]
    for h in range(H_B):
        o_ref[:, KV_LORA * h:KV_LORA * (h + 1)] = o[h * TQM:(h + 1) * TQM]


def _prompt_mla(qm, mlab):
    nb, rows, t = mlab.shape
    ni = t // TQM
    return pl.pallas_call(
        _prompt_mla_kernel,
        grid=(nb, ni),
        in_specs=[pl.BlockSpec((TQM, H_B * 2 * LANE), lambda b, i: (b * ni + i, 0)),
                  pl.BlockSpec((1, rows, t), lambda b, i: (b, 0, 0))],
        out_specs=pl.BlockSpec((TQM, H_B * KV_LORA), lambda b, i: (b * ni + i, 0)),
        out_shape=jax.ShapeDtypeStruct((nb * t, H_B * KV_LORA), F32),
        scratch_shapes=[pltpu.VMEM((H_B * TQM, t), F32), pltpu.VMEM((H_B * TQM, rows), F32)],
        compiler_params=pltpu.CompilerParams(vmem_limit_bytes=VMEM_LIMIT),
        name="prompt_mla",
    )(qm, mlab)


def _merge_kernel(x_ref, gate_ref, oa_ref, ol_ref, sza_ref, szb_ref, wuv_ref, wout_ref, gf_ref, y_ref):
    ya = oa_ref[...] * sza_ref[...]
    ol = ol_ref[...]
    szb = szb_ref[...]
    yb = [_dot(ol[:, KV_LORA * h:KV_LORA * (h + 1)].astype(BF16), wuv_ref[h]) * szb[:, V_HEAD * h:V_HEAD * (h + 1)]
          for h in range(H_B)]
    yab = jnp.concatenate([ya] + yb, axis=1).astype(BF16)
    xn = x_ref[...] + gate_ref[0] * _dot(yab, wout_ref[...])
    y_ref[...] = xn * lax.rsqrt(jnp.mean(xn * xn, axis=-1, keepdims=True) + EPS) * gf_ref[...]


def _merge(xf, gate3, per_row_mod, tb, oa, ol, sza, szb, wuv, wout, gfin):
    n, d = xf.shape
    tm = _row_tile(tb)
    tiles_b = tb // tm
    if per_row_mod:
        mod_spec = pl.BlockSpec((1, tm, d), lambda r: (r, 0, 0))
    else:
        mod_spec = pl.BlockSpec((1, 1, d), lambda r: (r // tiles_b, 0, 0))

    def full(a):
        return pl.BlockSpec(a.shape, lambda r, _n=a.ndim: (0,) * _n)

    def tmaj(cols):
        return pl.BlockSpec((tm, cols), lambda r: (r, 0))

    return pl.pallas_call(
        _merge_kernel,
        grid=(n // tm,),
        in_specs=[tmaj(d), mod_spec, tmaj(W_A), tmaj(W_B), tmaj(W_A), tmaj(W_B), full(wuv), full(wout), full(gfin)],
        out_specs=tmaj(d),
        out_shape=jax.ShapeDtypeStruct((n, d), F32),
        compiler_params=pltpu.CompilerParams(vmem_limit_bytes=VMEM_LIMIT),
        name="out_merge",
    )(xf, gate3, oa, ol, sza, szb, wuv, wout, gfin)


def _row_head_bias(bucket, tab_ref, g, shift):
    ts = bucket.shape[0] // R_A
    row_r = lax.broadcasted_iota(jnp.int32, bucket.shape, 0) // ts
    out = jnp.zeros(bucket.shape, F32)
    for r in range(R_A):
        h = R_A * g + r
        val = _bias_lookup(bucket, tab_ref, h)
        if shift:
            val = val - tab_ref[N_BUCKETS - 1, h]
        out = jnp.where(row_r == r, val, out)
    return out


def _own_lanes(x, b, ts):
    per_block = LANE // ts
    shift = (LANE - ts * (b % per_block)) % LANE
    lane = lax.broadcasted_iota(jnp.int32, x.shape, 1)
    return jnp.where(lane < ts, pltpu.roll(x, shift, 1), 0.0)


def _decode_nsa_kernel(pt_ref, tab_ref, *refs, ts, past, pps):
    pages = refs[:pps]
    (qa_ref, nsel_ref, nwin_ref, win_ref, gsel_ref, lperm_ref, wc_ref, pe_ref, wflat_ref, mov_ref,
     o_ref, wout_ref, a_ref, stash_ref, bc_ref, ds_ref, dn_ref, bw_ref, bwn_ref) = refs[pps:]
    del pt_ref
    b = pl.program_id(0)
    s = pl.program_id(1)
    n_steps = pl.num_programs(1)
    nch = past // CMP_STRIDE
    nc = nch - 1
    nrow = R_A * ts
    n_cmp_rows = 2 * G_A * HD_A
    wb = win_ref.shape[2]

    for pp in range(pps // 2):
        pa = pages[2 * pp][0]
        pb = pages[2 * pp + 1][0]
        pair = s * (pps // 2) + pp
        both = jnp.concatenate([pa[0:n_cmp_rows], pb[0:n_cmp_rows]], axis=1).astype(BF16)
        tb = _chunk_rows(lperm_ref, both)
        c0 = pl.multiple_of(pair * 16, 16)
        for j in range(CMP_STRIDE):
            a_ref[j, pl.ds(c0, 16), :] = tb[16 * j:16 * (j + 1)]
        k0 = pl.multiple_of(pair * 256, 256)
        stash_ref[:, pl.ds(k0, 256)] = jnp.concatenate(
            [pa[n_cmp_rows:2 * n_cmp_rows], pb[n_cmp_rows:2 * n_cmp_rows]], axis=1).astype(BF16)

    @pl.when(s == n_steps - 1)
    def _():
        trow = lax.broadcasted_iota(jnp.int32, (nrow, 1), 0) % ts

        @pl.when(b == 0)
        def _():
            n = lax.broadcasted_iota(jnp.int32, (nrow, nch), 1)
            dist = past + trow - (n * CMP_STRIDE + CMP_BLOCK - 1)
            u = lax.broadcasted_iota(jnp.int32, (nrow, LANE), 1)
            dist_s = LANE + trow - u
            dist_n = trow - u
            kl = lax.broadcasted_iota(jnp.int32, (nrow, wb), 1)
            dist_w = wb + trow - kl
            ok_n = (dist_n >= 0) & (u < ts)
            for g in range(G_A):
                bc_ref[g] = jnp.where((dist >= 0) & (n < nc), _row_head_bias(_rel_bucket(dist), tab_ref, g, False), NEG)
                ds_ref[g] = _row_head_bias(_rel_bucket(dist_s), tab_ref, g, True)
                dn_ref[g] = jnp.where(ok_n, _row_head_bias(_rel_bucket(dist_n), tab_ref, g, True), NEG)
                bw_ref[g] = jnp.where(dist_w < WINDOW, _row_head_bias(_rel_bucket(dist_w), tab_ref, g, False), NEG)
                bwn_ref[g] = jnp.where(ok_n, _row_head_bias(_rel_bucket(dist_n), tab_ref, g, False), NEG)

        cmpk = _compress_chunks(a_ref, wc_ref, pe_ref, wflat_ref, nch).astype(BF16)
        new_sel = _own_lanes(nsel_ref[0], b, ts)
        new_win = _own_lanes(nwin_ref[0], b, ts)
        nsp = mov_ref.shape[1]
        nbp = past // SEL_BLOCK
        jj = lax.broadcasted_iota(jnp.int32, (nrow, nsp), 1)
        cc = (past + trow) // SEL_BLOCK
        o_cmps, tots = [], []
        for g in range(G_A):
            q2 = qa_ref[0, g]
            s_c = _dot_nt(q2, cmpk[:, HD_A * g:HD_A * (g + 1)]) + bc_ref[g]
            p_c = _softmax_rows(s_c).astype(BF16)
            o_cmps.append(_dot(p_c, cmpk[:, LANE + HD_A * g:LANE + HD_A * (g + 1)]))
            imp = _dot(p_c, mov_ref[...])
            tot = imp
            for r in range(1, R_A):
                tot = tot + pltpu.roll(imp, r * ts, 0)
            tots.append(tot)
        half = nrow // 2
        first = lax.broadcasted_iota(jnp.int32, (half, nsp), 0) < ts
        jj8, cc8 = jj[0:half], cc[0:half]
        forced = (jj8 == 0) | (jj8 == cc8) | (jj8 == cc8 - 1)
        mixed = jnp.where(first, tots[0][0:half], tots[1][0:half])
        score = jnp.where(forced, FORCE_SCORE, jnp.where(jj8 <= cc8, mixed, -jnp.inf))
        rank = _rank(score, nbp + 1, 1)
        rank_sw = pltpu.roll(rank, ts, 0)
        allowed = jj8 <= cc8
        sel8 = [((jnp.where(first, rank, rank_sw) < N_SEL) & allowed).astype(F32)[:, 0:nbp],
                ((jnp.where(first, rank_sw, rank) < N_SEL) & allowed).astype(F32)[:, 0:nbp]]
        low = lax.broadcasted_iota(jnp.int32, (half, LANE), 1) < SEL_BLOCK
        madds = []
        for g in range(G_A):
            flags = [jnp.where(low, sel8[g][:, 2 * c:2 * c + 1], sel8[g][:, 2 * c + 1:2 * c + 2])
                     for c in range(nbp // 2)]
            m8 = (jnp.concatenate(flags, axis=1) - 1.0) * (-NEG)
            madds.append(jnp.concatenate([m8, m8], axis=0))
        for g in range(G_A):
            q2 = qa_ref[0, g]
            o_cmp = o_cmps[g]
            s_p = _dot(q2, stash_ref[HD_A * g:HD_A * (g + 1), :]) + madds[g]
            s_last = s_p[:, past - LANE:past] + ds_ref[g]
            k_new = new_sel[HD_A * g:HD_A * (g + 1)].astype(BF16)
            v_new = new_sel[LANE + HD_A * g:LANE + HD_A * (g + 1)].astype(BF16)
            s_n = _dot(q2, k_new) + dn_ref[g]
            s_all = jnp.concatenate([s_p[:, 0:past - LANE], s_last, s_n], axis=1)
            m = jnp.max(s_all, axis=-1, keepdims=True)
            p = jnp.exp(s_all - m)
            l = jnp.sum(p, axis=-1, keepdims=True)
            pb16 = p.astype(BF16)
            o_sel = (_dot_nt(pb16[:, 0:past], stash_ref[LANE + HD_A * g:LANE + HD_A * (g + 1), :])
                     + _dot_nt(pb16[:, past:past + LANE], v_new)) / l
            kw = win_ref[0, HD_A * g:HD_A * (g + 1), :].astype(BF16)
            vw = win_ref[0, LANE + HD_A * g:LANE + HD_A * (g + 1), :].astype(BF16)
            kw_new = new_win[HD_A * g:HD_A * (g + 1)].astype(BF16)
            vw_new = new_win[LANE + HD_A * g:LANE + HD_A * (g + 1)].astype(BF16)
            s_w = jnp.concatenate([_dot(q2, kw) + bw_ref[g], _dot(q2, kw_new) + bwn_ref[g]], axis=1)
            p_w = _softmax_rows(s_w).astype(BF16)
            o_win = _dot_nt(p_w[:, 0:wb], vw) + _dot_nt(p_w[:, wb:wb + LANE], vw_new)
            gs = gsel_ref[0, g]
            o_ref[0, g] = gs[:, 0:1] * o_cmp + gs[:, 1:2] * o_sel + gs[:, 2:3] * o_win

        win = win_ref[0]
        rolled = pltpu.roll(win, wb - ts, 1)
        new_w = pltpu.roll(new_win, LANE - ts, 1)
        lane = lax.broadcasted_iota(jnp.int32, (win.shape[0], LANE), 1)
        last = jnp.where(lane >= LANE - ts, new_w, rolled[:, wb - LANE:wb])
        wout_ref[0] = jnp.concatenate([rolled[:, 0:wb - LANE], last], axis=1)


def _decode_nsa(page_table, rel_bias, cache_t, qa_s, nsat_s, wint_s, win_t, gsel, cw, mov, ts, past):
    bd, n_pages = page_table.shape
    pps = math.gcd(n_pages, PAGES_PER_STEP)
    n_steps = n_pages // pps
    lperm, wc, pe, wflat = cw
    nrow = R_A * ts
    nch = past // CMP_STRIDE
    wb = win_t.shape[2]
    feat = cache_t.shape[1]
    per_block = LANE // ts

    def full(a):
        return pl.BlockSpec(a.shape, lambda b, s, pt, _n=a.ndim: (0,) * _n, pipeline_mode=pl.Buffered(1))

    def page_spec(k):
        return pl.BlockSpec((1, feat, LANE), lambda b, s, pt, _k=k: (pt[b, s * pps + _k], 0, 0))

    in_specs = ([pl.BlockSpec(memory_space=pltpu.SMEM)]
                + [page_spec(k) for k in range(pps)]
                + [pl.BlockSpec((1, G_A, nrow, HD_A), lambda b, s, pt: (b, 0, 0, 0)),
                   pl.BlockSpec((1, 2 * LANE, LANE), lambda b, s, pt: (0, 1, b // per_block)),
                   pl.BlockSpec((1, 2 * LANE, LANE), lambda b, s, pt: (0, 0, b // per_block)),
                   pl.BlockSpec((1, 2 * LANE, wb), lambda b, s, pt: (b, 0, 0)),
                   pl.BlockSpec((1, G_A, nrow, LANE), lambda b, s, pt: (b, 0, 0, 0)),
                   full(lperm), full(wc), full(pe), full(wflat), full(mov)])
    grid_spec = pltpu.PrefetchScalarGridSpec(
        num_scalar_prefetch=1,
        grid=(bd, n_steps),
        in_specs=in_specs,
        out_specs=[pl.BlockSpec((1, G_A, nrow, HD_A), lambda b, s, pt: (b, 0, 0, 0)),
                   pl.BlockSpec((1, 2 * LANE, wb), lambda b, s, pt: (b, 0, 0))],
        scratch_shapes=[pltpu.VMEM((CMP_STRIDE, nch, 2 * LANE), BF16),
                        pltpu.VMEM((2 * LANE, past), BF16),
                        pltpu.VMEM((G_A, nrow, nch), F32),
                        pltpu.VMEM((G_A, nrow, LANE), F32),
                        pltpu.VMEM((G_A, nrow, LANE), F32),
                        pltpu.VMEM((G_A, nrow, wb), F32),
                        pltpu.VMEM((G_A, nrow, LANE), F32)])
    return pl.pallas_call(
        functools.partial(_decode_nsa_kernel, ts=ts, past=past, pps=pps),
        grid_spec=grid_spec,
        out_shape=(jax.ShapeDtypeStruct((bd, G_A, nrow, HD_A), F32),
                   jax.ShapeDtypeStruct((bd, 2 * LANE, wb), F32)),
        compiler_params=pltpu.CompilerParams(vmem_limit_bytes=VMEM_LIMIT,
                                             dimension_semantics=("arbitrary", "arbitrary")),
        name="decode_nsa",
    )(page_table, rel_bias, *([cache_t] * pps), qa_s, nsat_s, wint_s, win_t, gsel,
      lperm, wc, pe, wflat, mov)


def _decode_mla_kernel(pt_ref, *refs, ts, n_pages):
    pages = refs[:n_pages]
    qm_ref, new_ref, o_ref = refs[n_pages:]
    del pt_ref
    nrow = H_B * ts
    q2 = qm_ref[0]
    feat = pages[0].shape[1]
    zpad = jnp.zeros((2 * LANE - feat, LANE), BF16)
    keys, scores = [], []
    for grp in range(n_pages // MLA_PAGE_GROUP):
        kt = jnp.concatenate(
            [jnp.concatenate([pages[grp * MLA_PAGE_GROUP + k][0].astype(BF16), zpad], axis=0)
             for k in range(MLA_PAGE_GROUP)], axis=1)
        keys.append(kt)
        scores.append(_dot(q2, kt))
    new = _own_lanes(new_ref[0], pl.program_id(0), ts)
    kn = jnp.concatenate([new.astype(BF16), zpad], axis=0)
    trow = lax.broadcasted_iota(jnp.int32, (nrow, LANE), 0) % ts
    u = lax.broadcasted_iota(jnp.int32, (nrow, LANE), 1)
    scores.append(_dot(q2, kn) + jnp.where((u <= trow) & (u < ts), 0.0, NEG))
    s_all = jnp.concatenate(scores, axis=1)
    m = jnp.max(s_all, axis=-1, keepdims=True)
    p = jnp.exp(s_all - m)
    l = jnp.sum(p, axis=-1, keepdims=True)
    pb = p.astype(BF16)
    gw = MLA_PAGE_GROUP * LANE
    acc = _dot_nt(pb[:, len(keys) * gw:len(keys) * gw + LANE], kn[0:KV_LORA])
    for grp, kt in enumerate(keys):
        acc = acc + _dot_nt(pb[:, grp * gw:(grp + 1) * gw], kt[0:KV_LORA])
    o_ref[0] = acc / l


def _decode_mla(page_table, cache_t, qm_s, new_mla, ts):
    bd, n_pages = page_table.shape
    nrow = H_B * ts
    feat = cache_t.shape[1]

    def page_spec(k):
        return pl.BlockSpec((1, feat, LANE), lambda b, pt, _k=k: (pt[b, _k], 0, 0))

    grid_spec = pltpu.PrefetchScalarGridSpec(
        num_scalar_prefetch=1,
        grid=(bd,),
        in_specs=([page_spec(k) for k in range(n_pages)]
                  + [pl.BlockSpec((1, nrow, 2 * LANE), lambda b, pt: (b, 0, 0)),
                     pl.BlockSpec((1, feat, LANE), lambda b, pt: (0, 0, b // (LANE // ts)))]),
        out_specs=pl.BlockSpec((1, nrow, KV_LORA), lambda b, pt: (b, 0, 0)))
    return pl.pallas_call(
        functools.partial(_decode_mla_kernel, ts=ts, n_pages=n_pages),
        grid_spec=grid_spec,
        out_shape=jax.ShapeDtypeStruct((bd, nrow, KV_LORA), F32),
        compiler_params=pltpu.CompilerParams(vmem_limit_bytes=VMEM_LIMIT,
                                             dimension_semantics=("arbitrary",)),
        name="decode_mla",
    )(page_table, *([cache_t] * n_pages), qm_s, new_mla)


def _rope_tables(pos):
    half = ROPE_DIM // 2
    inv = ROPE_BASE ** (-jnp.arange(half, dtype=F32) / half)
    ang = pos.astype(F32)[:, None] * inv[None, :]
    cos, sin = jnp.cos(ang), jnp.sin(ang)
    z = jnp.zeros((pos.shape[0], LANE - ROPE_DIM), F32)
    cos_tm = jnp.tile(jnp.concatenate([cos, cos, z], axis=1), (1, H_B))
    sin_tm = jnp.tile(jnp.concatenate([sin, sin, z], axis=1), (1, H_B))
    return cos_tm, sin_tm, cos.T, sin.T


def _overlap_t(n_cmp_rows, n_sel):
    cs = jnp.arange(n_cmp_rows)[None, :] * CMP_STRIDE
    ss = jnp.arange(n_sel)[:, None] * SEL_BLOCK
    ov = jnp.minimum(cs + CMP_BLOCK, ss + SEL_BLOCK) - jnp.maximum(cs, ss)
    ov = jnp.maximum(ov, 0).astype(F32) / CMP_BLOCK
    return jnp.where(jnp.arange(n_cmp_rows)[None, :] < n_cmp_rows - 1, ov, 0.0)


def _compress_weights(w_k, w_v, pe_k, pe_v):
    eye = jnp.eye(G_A, dtype=F32)

    def pair_blocks(w):
        w6 = w.reshape(2, CMP_STRIDE // 2, 2, HD_A, HD_A)
        blk = jnp.einsum('hpjde,gk->pjgdhke', w6, eye)
        return blk.reshape(CMP_STRIDE // 2, 2 * LANE, 2 * LANE)

    wc = jnp.stack([pair_blocks(w_k), pair_blocks(w_v)]).astype(BF16)
    pe = jnp.stack([jnp.tile(pe_k.reshape(1, -1), (8, 1)), jnp.tile(pe_v.reshape(1, -1), (8, 1))]).astype(BF16)
    wflat = jnp.stack([jnp.tile(w_k.reshape(-1, HD_A), (1, G_A)),
                       jnp.tile(w_v.reshape(-1, HD_A), (1, G_A))]).astype(BF16)
    r = jnp.arange(256)
    lperm = (jnp.arange(256)[None, :] == ((r % 16) * 16 + r // 16)[:, None]).astype(BF16)
    return lperm, wc, pe, wflat


def kernel(x_prompt, x_sample, cache_nsa, cache_mla, state_win, page_table, c_prompt, c_sample,
           w_ada, b_ada, g_norm, w_in, w_cmp_k, w_cmp_v, pe_cmp_k, pe_cmp_v, rel_bias,
           g_q_lat, w_uq, g_kv_lat, w_uk, w_uv, w_out, g_final):
    assert w_in.shape[0] == 1, "one layer"
    nb, t, d = x_prompt.shape
    bd, ts, _ = x_sample.shape
    n_pages = page_table.shape[1]
    page = cache_nsa.shape[2]
    past = n_pages * page
    assert page == LANE and (past + ts) // CMP_STRIDE == past // CMP_STRIDE and ts <= SEL_BLOCK
    assert (R_A * ts // 2) % ts == 0 and t % TQM == 0 and LANE % ts == 0 and (bd * ts) % LANE == 0
    assert t % KT == 0 and (bd * ts) % (TM // 2) == 0 and n_pages % MLA_PAGE_GROUP == 0 and n_pages % 2 == 0
    wb = state_win.shape[2]

    wi = w_in[0]
    o1 = W_A
    o2 = o1 + 6 * G_A * HD_A
    o3 = o2 + 3 * H_A
    o4 = o3 + W_A
    o5 = o4 + Q_LORA
    o6 = o5 + KV_LORA
    o7 = o6 + ROPE_DIM
    wg = jnp.pad(wi[:, o2:o3], ((0, 0), (0, LANE - 3 * H_A)))
    wt = jnp.concatenate([wi[:, 0:o1] * (HD_A ** -0.5), wi[:, o3:o4], wi[:, o7:], wi[:, o4:o5], wg], axis=1).astype(BF16)
    wf = jnp.concatenate([wi[:, o1:o2], wi[:, o5:o6], wi[:, o6:o7]], axis=1).T.astype(BF16)
    wq = w_uq[0]
    wqn = wq[:, :, :NOPE].reshape(Q_LORA, H_B * NOPE).astype(BF16)
    wr = wq[:, :, NOPE:]
    half = ROPE_DIM // 2
    wsw = jnp.concatenate([-wr[..., half:], wr[..., :half]], axis=-1)
    padr = ((0, 0), (0, 0), (0, LANE - ROPE_DIM))
    wqr = jnp.pad(wr, padr).reshape(Q_LORA, H_B * LANE).astype(BF16)
    wqs = jnp.pad(wsw, padr).reshape(Q_LORA, H_B * LANE).astype(BF16)
    wuk = jnp.transpose(w_uk[0], (1, 2, 0)).astype(BF16)
    wuv = jnp.transpose(w_uv[0], (1, 0, 2)).astype(BF16)
    wts = (g_norm[0][None], wt, wf, g_q_lat[0][None], wqn, wqr, wqs, wuk, g_kv_lat[0][:, None])
    cw = _compress_weights(w_cmp_k[0], w_cmp_v[0], pe_cmp_k[0], pe_cmp_v[0])
    wout = w_out[0].astype(BF16)
    gfin = g_final[None]

    n_c = nb + bd
    n_cp = -(-n_c // 8) * 8
    c_all = jnp.pad(jnp.concatenate([c_prompt, c_sample], axis=0), ((0, n_cp - n_c), (0, 0)))
    mod = _modulation(c_all, w_ada[0].astype(BF16), b_ada)
    shift, scale, gate = mod[:, 0:d], mod[:, d:2 * d], mod[:, 2 * d:3 * d]

    def prompt_mod(a):
        return a[0:nb][:, None, :]

    def sample_mod(a):
        tm_s = _row_tile(bd * ts)
        return jnp.repeat(a[nb:nb + bd], ts, axis=0).reshape(bd * ts // tm_s, tm_s, d)

    pos_p = jnp.arange(t, dtype=jnp.int32)
    xp = x_prompt.reshape(nb * t, d)
    (qa, gates, sza, szb, qm, nsat, wint, kvb, mlat, mlab) = _project(
        xp, prompt_mod(shift), prompt_mod(scale), False, nb, t, wts, *_rope_tables(pos_p))
    cmpk = _prompt_compress(kvb, cw)
    nch_p = t // CMP_STRIDE
    ns_p = t // SEL_BLOCK
    movt = _overlap_t(nch_p, ns_p).astype(BF16)
    emat = jnp.where(jnp.arange(t)[None, :] // SEL_BLOCK == jnp.arange(ns_p)[:, None], NEG, 0.0).astype(BF16)
    oa = _prompt_nsa(rel_bias, qa, kvb, cmpk, gates, movt, emat)
    ol = _prompt_mla(qm, mlab)
    y_prompt = _merge(xp, prompt_mod(gate), False, t, oa, ol, sza, szb, wuv, wout, gfin).reshape(nb, t, d)

    n_s = bd * ts
    pos_s = past + jnp.arange(ts, dtype=jnp.int32)
    cos_tm, sin_tm, cos_t, sin_t = _rope_tables(pos_s)
    xs = x_sample.reshape(n_s, d)
    (qa_s, gates_s, sza_s, szb_s, qm_s, nsat_s, wint_s, _, mlat_s, _) = _project(
        xs, sample_mod(shift), sample_mod(scale), True, 1, n_s, wts,
        jnp.tile(cos_tm, (bd, 1)), jnp.tile(sin_tm, (bd, 1)), jnp.tile(cos_t, (1, bd)), jnp.tile(sin_t, (1, bd)))
    nrow = R_A * ts
    qa_g = qa_s.reshape(bd, ts, G_A, R_A, HD_A).transpose(0, 2, 3, 1, 4).reshape(bd, G_A, nrow, HD_A)
    gsel = gates_s[:, 0:3 * H_A].reshape(bd, ts, G_A, R_A, 3).transpose(0, 2, 3, 1, 4).reshape(bd, G_A, nrow, 3)
    gsel = jnp.pad(gsel, ((0, 0), (0, 0), (0, 0), (0, LANE - 3)))
    qm_h = qm_s.reshape(bd, ts, H_B, 2 * LANE).transpose(0, 2, 1, 3).reshape(bd, H_B * ts, 2 * LANE)
    n_phys = cache_nsa.shape[1]
    cache_nsa_t = jnp.transpose(cache_nsa[0], (0, 2, 3, 4, 1)).reshape(n_phys, 4 * G_A * HD_A, page)
    cache_mla_t = jnp.transpose(cache_mla[0], (0, 2, 1))
    win_t = jnp.transpose(state_win[0], (0, 2, 3, 4, 1)).reshape(bd, 2 * G_A * HD_A, wb)
    nch_s = past // CMP_STRIDE
    nbp = past // SEL_BLOCK
    nsp = -(-(nbp + 1) // LANE) * LANE
    mov_s = jnp.pad(_overlap_t(nch_s, nbp + 1).T, ((0, 0), (0, nsp - nbp - 1))).astype(BF16)
    oa_g, win_new = _decode_nsa(page_table, rel_bias, cache_nsa_t, qa_g, nsat_s, wint_s, win_t, gsel, cw,
                                mov_s, ts, past)
    ol_h = _decode_mla(page_table, cache_mla_t, qm_h, mlat_s, ts)
    oa_s = oa_g.reshape(bd, G_A, R_A, ts, HD_A).transpose(0, 3, 1, 2, 4).reshape(n_s, W_A)
    ol_s = ol_h.reshape(bd, H_B, ts, KV_LORA).transpose(0, 2, 1, 3).reshape(n_s, H_B * KV_LORA)
    y_sample = _merge(xs, sample_mod(gate), True, n_s, oa_s, ol_s, sza_s, szb_s, wuv, wout, gfin).reshape(bd, ts, d)

    new_nsa_prompt = nsat.reshape(nb, 4, G_A, HD_A, t).transpose(0, 4, 1, 2, 3)[None]
    new_nsa_sample = nsat_s[0].reshape(4, G_A, HD_A, bd, ts).transpose(3, 4, 0, 1, 2)[None]
    new_mla_prompt = mlat.transpose(0, 2, 1)[None]
    new_mla_sample = mlat_s[0].reshape(KV_LORA + ROPE_DIM, bd, ts).transpose(1, 2, 0)[None]
    wlen = min(WINDOW, t)
    new_win_prompt = wint[:, :, t - wlen:].reshape(nb, 2, G_A, HD_A, wlen).transpose(0, 4, 1, 2, 3)[None]
    new_win_sample = win_new.reshape(bd, 2, G_A, HD_A, wb).transpose(0, 4, 1, 2, 3)[None]
    return (y_prompt, y_sample, new_nsa_prompt, new_nsa_sample, new_mla_prompt, new_mla_sample,
            new_win_prompt, new_win_sample)
```

```python
import functools
import math

import jax
import jax.numpy as jnp
from jax import lax
from jax.experimental import pallas as pl
from jax.experimental.pallas import tpu as pltpu

F32 = jnp.float32
BF16 = jnp.bfloat16

H_A, G_A, R_A, HD_A = 8, 2, 4, 64
W_A = H_A * HD_A
CMP_BLOCK, CMP_STRIDE = 32, 16
SEL_BLOCK, N_SEL = 64, 16
WINDOW = 512
FORCE_SCORE = 1e4
H_B, NOPE, ROPE_DIM, V_HEAD, Q_LORA, KV_LORA = 4, 128, 64, 128, 256, 128
W_B = H_B * V_HEAD
ROPE_BASE = 10000.0
N_BUCKETS, MAX_DIST = 32, 128
EPS = 1e-6
NEG = -1e30

TQ = 128
TQM = 256
KT = 1024
TM = 512
LANE = 128
MLA_PAGE_GROUP = 16
VMEM_LIMIT = 56 * 1024 * 1024


def _row_tile(rows):
    return TM if rows % TM == 0 else TM // 2


def _dot(a, b):
    return jnp.dot(a, b, preferred_element_type=F32)


def _dot_nt(a, b):
    return lax.dot_general(a, b, (((1,), (1,)), ((), ())), preferred_element_type=F32)


def _rel_bucket(dist):
    n = jnp.maximum(dist, 0)
    max_exact = N_BUCKETS // 2
    span = N_BUCKETS - max_exact
    large = jnp.full(n.shape, max_exact, jnp.int32)
    for k in range(1, span):
        thr = math.ceil(max_exact * (MAX_DIST / max_exact) ** (k / span))
        large = large + (n >= thr).astype(jnp.int32)
    return jnp.where(n < max_exact, n, large)


def _bias_lookup(bucket, tab_ref, h):
    acc = jnp.zeros(bucket.shape, F32)
    for k in range(N_BUCKETS):
        acc = jnp.where(bucket == k, tab_ref[k, h], acc)
    return acc


def _softmax_rows(s):
    m = jnp.max(s, axis=-1, keepdims=True)
    p = jnp.exp(s - m)
    l = jnp.sum(p, axis=-1, keepdims=True)
    return jnp.where(s > 0.5 * NEG, p / l, 0.0)


def _mod_kernel(c_ref, w_ref, b_ref, o_ref):
    o_ref[...] = _dot(c_ref[...].astype(BF16), w_ref[...].astype(BF16)) + b_ref[...]


def _modulation(c, w_ada, b_ada):
    n, d = c.shape
    n3 = w_ada.shape[1]
    tn = 512
    return pl.pallas_call(
        _mod_kernel,
        grid=(n3 // tn,),
        in_specs=[pl.BlockSpec((n, d), lambda j: (0, 0)),
                  pl.BlockSpec((d, tn), lambda j: (0, j)),
                  pl.BlockSpec((1, tn), lambda j: (0, j))],
        out_specs=pl.BlockSpec((n, tn), lambda j: (0, j)),
        out_shape=jax.ShapeDtypeStruct((n, n3), F32),
        name="adaln_mod",
    )(c, w_ada, b_ada)


def _proj_kernel(x_ref, shift_ref, scale_ref, gn_ref, wt_ref, wf_ref, gq_ref, wqn_ref, wqr_ref,
                 wqs_ref, wuk_ref, gkv_ref, cos_ref, sin_ref, cost_ref, sint_ref,
                 qa_ref, gates_ref, sza_ref, szb_ref, qm_ref, nsat_ref, wint_ref, kvb_ref,
                 mlat_ref, mlab_ref, *, mla_scale):
    x = x_ref[...]
    y = x * lax.rsqrt(jnp.mean(x * x, axis=-1, keepdims=True) + EPS) * gn_ref[...]
    h = y * (1.0 + scale_ref[0]) + shift_ref[0]
    hb = h.astype(BF16)

    ut = _dot(hb, wt_ref[...])
    qa_ref[...] = ut[:, 0:W_A].astype(BF16)
    za = ut[:, W_A:2 * W_A]
    zb = ut[:, 2 * W_A:3 * W_A]
    sza_ref[...] = za * jax.nn.sigmoid(za)
    szb_ref[...] = zb * jax.nn.sigmoid(zb)
    cq = ut[:, 3 * W_A:3 * W_A + Q_LORA]
    gates_ref[...] = jax.nn.sigmoid(ut[:, 3 * W_A + Q_LORA:3 * W_A + Q_LORA + LANE])

    cqn = (cq * lax.rsqrt(jnp.mean(cq * cq, axis=-1, keepdims=True) + EPS) * gq_ref[...]).astype(BF16)
    qn = _dot(cqn, wqn_ref[...])
    qr = _dot(cqn, wqr_ref[...])
    qs = _dot(cqn, wqs_ref[...])
    cos_q = jnp.concatenate([cos_ref[...]] * H_B, axis=1)
    sin_q = jnp.concatenate([sin_ref[...]] * H_B, axis=1)
    q_rope = qr * cos_q + qs * sin_q
    for hh in range(H_B):
        q_lat = _dot(qn[:, hh * NOPE:(hh + 1) * NOPE].astype(BF16), wuk_ref[hh])
        qm_ref[:, 2 * LANE * hh:2 * LANE * hh + LANE] = (q_lat * mla_scale).astype(BF16)
        qm_ref[:, 2 * LANE * hh + LANE:2 * LANE * (hh + 1)] = (
            q_rope[:, hh * LANE:(hh + 1) * LANE] * mla_scale).astype(BF16)

    uf = _dot_nt(wf_ref[...], hb)
    n_sel_rows = 4 * G_A * HD_A
    n_kv_rows = 6 * G_A * HD_A
    nsat_ref[0] = uf[0:n_sel_rows]
    wint_ref[0] = uf[n_sel_rows:n_kv_rows]
    kvb_ref[0] = uf[0:n_kv_rows].astype(BF16)
    c = uf[n_kv_rows:n_kv_rows + KV_LORA]
    cn = c * lax.rsqrt(jnp.mean(c * c, axis=0, keepdims=True) + EPS) * gkv_ref[...]
    kr = uf[n_kv_rows + KV_LORA:n_kv_rows + KV_LORA + ROPE_DIM]
    half = ROPE_DIM // 2
    x1, x2 = kr[0:half], kr[half:ROPE_DIM]
    ct, st = cost_ref[...], sint_ref[...]
    mla = jnp.concatenate([cn, x1 * ct - x2 * st, x1 * st + x2 * ct], axis=0)
    mlat_ref[0] = mla
    n_pad = 2 * LANE - KV_LORA - ROPE_DIM
    pad = jnp.where(lax.broadcasted_iota(jnp.int32, (n_pad, mla.shape[1]), 0) == 0, 1.0, 0.0)
    mlab_ref[0] = jnp.concatenate([mla, pad], axis=0).astype(BF16)


def _project(xf, shift3, scale3, per_row_mod, nb, tb, wts, cos_tm, sin_tm, cos_t, sin_t):
    n, d = xf.shape
    tm = _row_tile(tb)
    tiles_b = tb // tm
    ntab = cos_tm.shape[0] // tm
    (gn, wt, wf, gq, wqn, wqr, wqs, wuk, gkv) = wts
    if per_row_mod:
        mod_spec = pl.BlockSpec((1, tm, d), lambda r: (r, 0, 0))
    else:
        mod_spec = pl.BlockSpec((1, 1, d), lambda r: (r // tiles_b, 0, 0))

    def full(a):
        return pl.BlockSpec(a.shape, lambda r, _n=a.ndim: (0,) * _n)

    def fm(rows):
        return pl.BlockSpec((1, rows, tm), lambda r: (r // tiles_b, 0, r % tiles_b))

    def tmaj(cols):
        return pl.BlockSpec((tm, cols), lambda r: (r, 0))

    n_sel_rows = 4 * G_A * HD_A
    out_shape = (
        jax.ShapeDtypeStruct((n, W_A), BF16),
        jax.ShapeDtypeStruct((n, LANE), F32),
        jax.ShapeDtypeStruct((n, W_A), F32),
        jax.ShapeDtypeStruct((n, W_B), F32),
        jax.ShapeDtypeStruct((n, H_B * 2 * LANE), BF16),
        jax.ShapeDtypeStruct((nb, n_sel_rows, tb), F32),
        jax.ShapeDtypeStruct((nb, 2 * G_A * HD_A, tb), F32),
        jax.ShapeDtypeStruct((nb, 6 * G_A * HD_A, tb), BF16),
        jax.ShapeDtypeStruct((nb, KV_LORA + ROPE_DIM, tb), F32),
        jax.ShapeDtypeStruct((nb, 2 * LANE, tb), BF16),
    )
    out_specs = (tmaj(W_A), tmaj(LANE), tmaj(W_A), tmaj(W_B), tmaj(H_B * 2 * LANE),
                 fm(n_sel_rows), fm(2 * G_A * HD_A), fm(6 * G_A * HD_A), fm(KV_LORA + ROPE_DIM), fm(2 * LANE))
    in_specs = [tmaj(d), mod_spec, mod_spec, full(gn), full(wt), full(wf), full(gq), full(wqn),
                full(wqr), full(wqs), full(wuk), full(gkv),
                pl.BlockSpec((tm, LANE), lambda r: (r % ntab, 0)),
                pl.BlockSpec((tm, LANE), lambda r: (r % ntab, 0)),
                pl.BlockSpec((ROPE_DIM // 2, tm), lambda r: (0, r % ntab)),
                pl.BlockSpec((ROPE_DIM // 2, tm), lambda r: (0, r % ntab))]
    return pl.pallas_call(
        functools.partial(_proj_kernel, mla_scale=(NOPE + ROPE_DIM) ** -0.5),
        grid=(n // tm,),
        in_specs=in_specs,
        out_specs=out_specs,
        out_shape=out_shape,
        compiler_params=pltpu.CompilerParams(vmem_limit_bytes=VMEM_LIMIT),
        name="in_proj",
    )(xf, shift3, scale3, gn, wt, wf, gq, wqn, wqr, wqs, wuk, gkv, cos_tm, sin_tm, cos_t, sin_t)


def _chunk_rows(lperm_ref, feats_tok):
    return _dot_nt(lperm_ref[...], feats_tok).astype(BF16)


def _compress_chunks(a_ref, wc_ref, pe_ref, wflat_ref, nch):
    outs = []
    for kind in range(2):
        acc = jnp.zeros((nch, 2 * LANE), F32)
        for p in range(CMP_STRIDE // 2):
            lhs = jnp.concatenate([a_ref[2 * p, :, kind * LANE:(kind + 1) * LANE],
                                   a_ref[2 * p + 1, :, kind * LANE:(kind + 1) * LANE]], axis=1)
            acc = acc + _dot(lhs, wc_ref[kind, p])
        first, second = acc[:, 0:LANE], acc[:, LANE:2 * LANE]
        cb = _dot(pe_ref[kind], wflat_ref[kind])[0:1]
        outs.append(first + pltpu.roll(second, nch - 1, 0) + cb)
    return jnp.concatenate(outs, axis=1)


def _prompt_compress_kernel(kv_ref, lperm_ref, wc_ref, pe_ref, wflat_ref, o_ref, a_ref):
    t = kv_ref.shape[2]
    nch = t // CMP_STRIDE
    for tix in range(t // 256):
        tb = _chunk_rows(lperm_ref, kv_ref[0, :, tix * 256:(tix + 1) * 256])
        for j in range(CMP_STRIDE):
            a_ref[j, tix * 16:(tix + 1) * 16, :] = tb[16 * j:16 * (j + 1)]
    o_ref[0] = _compress_chunks(a_ref, wc_ref, pe_ref, wflat_ref, nch).astype(BF16)


def _prompt_compress(kvb, cw):
    nb, _, t = kvb.shape
    nch = t // CMP_STRIDE
    lperm, wc, pe, wflat = cw

    def full(a):
        return pl.BlockSpec(a.shape, lambda b, _n=a.ndim: (0,) * _n)

    return pl.pallas_call(
        _prompt_compress_kernel,
        grid=(nb,),
        in_specs=[pl.BlockSpec((1, 2 * LANE, t), lambda b: (b, 0, 0)), full(lperm), full(wc), full(pe), full(wflat)],
        out_specs=pl.BlockSpec((1, nch, 2 * LANE), lambda b: (b, 0, 0)),
        out_shape=jax.ShapeDtypeStruct((nb, nch, 2 * LANE), BF16),
        scratch_shapes=[pltpu.VMEM((CMP_STRIDE, nch, 2 * LANE), BF16)],
        compiler_params=pltpu.CompilerParams(vmem_limit_bytes=VMEM_LIMIT),
        name="prompt_compress",
    )(kvb, lperm, wc, pe, wflat)


def _rank(score, n_valid, axis):
    size = 8 if axis == 0 else LANE
    ntile = score.shape[axis] // size

    def tile(a, k):
        return a[k * size:(k + 1) * size, :] if axis == 0 else a[:, k * size:(k + 1) * size]

    tiles = [tile(score, k) for k in range(ntile)]
    ranks = [jnp.zeros(tiles[0].shape, F32) for _ in range(ntile)]
    pos = lax.broadcasted_iota(jnp.int32, tiles[0].shape, axis)
    for j2 in range(n_valid):
        kt, off = divmod(j2, size)
        cand = tiles[kt][off:off + 1, :] if axis == 0 else tiles[kt][:, off:off + 1]
        for k in range(ntile):
            if k > kt:
                before = cand >= tiles[k]
            elif k < kt:
                before = cand > tiles[k]
            else:
                before = (cand > tiles[k]) | ((cand == tiles[k]) & (pos > off))
            ranks[k] = ranks[k] + jnp.where(before, 1.0, 0.0)
    return jnp.concatenate(ranks, axis=axis)


def _prompt_nsa_kernel(tab_ref, qa_ref, sel_ref, win_ref, cmp_ref, gates_ref, movt_ref, eneg_ref,
                       o_ref, bwin_ref, dsel_ref, gcmp_ref, dcmp_ref, sc_ref):
    i = pl.program_id(0)
    b = pl.program_id(1)
    t = sel_ref.shape[2]
    nch = cmp_ref.shape[1]
    ns = movt_ref.shape[0]
    n_win_tiles = WINDOW // TQ + 1
    chunks_per_tile = TQ // CMP_STRIDE
    m_off = chunks_per_tile * (t // TQ - 1)

    @pl.when((i == 0) & (b == 0))
    def _():
        tl = lax.broadcasted_iota(jnp.int32, (TQ, n_win_tiles * TQ), 0)
        kl = lax.broadcasted_iota(jnp.int32, (TQ, n_win_tiles * TQ), 1)
        dist = tl + WINDOW - kl
        bucket = _rel_bucket(dist)
        ok = (dist >= 0) & (dist < WINDOW)
        tl2 = lax.broadcasted_iota(jnp.int32, (TQ, 2 * TQ), 0)
        kl2 = lax.broadcasted_iota(jnp.int32, (TQ, 2 * TQ), 1)
        dist2 = tl2 + TQ - kl2
        bucket2 = _rel_bucket(dist2)
        for h in range(H_A):
            bwin_ref[h] = jnp.where(ok, _bias_lookup(bucket, tab_ref, h), NEG)
            dsel_ref[h] = jnp.where(dist2 >= 0,
                                    _bias_lookup(bucket2, tab_ref, h) - tab_ref[N_BUCKETS - 1, h], NEG)
        tl3 = lax.broadcasted_iota(jnp.int32, (TQ, 2 * nch), 0)
        m3 = lax.broadcasted_iota(jnp.int32, (TQ, 2 * nch), 1)
        dist3 = tl3 - (m3 - m_off) * CMP_STRIDE - (CMP_BLOCK - 1)
        bucket3 = _rel_bucket(dist3)
        for h in range(H_A):
            gcmp_ref[h] = jnp.where(dist3 >= 0,
                                    _bias_lookup(bucket3, tab_ref, h) - tab_ref[N_BUCKETS - 1, h], NEG)

    @pl.when(b == 0)
    def _():
        shift = (2 * nch - (m_off - chunks_per_tile * i)) % (2 * nch)
        for h in range(H_A):
            dcmp_ref[h] = pltpu.roll(gcmp_ref[h], shift, 1)[:, 0:nch]

    i_prev = jnp.maximum(i - 1, 0)
    n_main = (i_prev * TQ + KT - 1) // KT
    c_a = pl.multiple_of(i_prev * TQ, TQ)
    c_b = pl.multiple_of(i * TQ, TQ)
    blocks_per_tile = TQ // SEL_BLOCK
    ones_rows = jnp.ones((16, KT), BF16)
    gates = gates_ref[...]

    def k_ext(g, c0, width):
        return jnp.concatenate([sel_ref[0, HD_A * g:HD_A * (g + 1), pl.ds(c0, width)],
                                eneg_ref[:, pl.ds(c0, width)]], axis=0)

    def v_ext(g, c0, width):
        return jnp.concatenate([sel_ref[0, LANE + HD_A * g:LANE + HD_A * (g + 1), pl.ds(c0, width)],
                                ones_rows[:, 0:width]], axis=0)

    q2s, o_cmps, lhs_mains, lhs_tails = [], [], [], []
    for g in range(G_A):
        q2 = jnp.concatenate([qa_ref[:, HD_A * (R_A * g + r):HD_A * (R_A * g + r + 1)] for r in range(R_A)],
                             axis=0)
        kc = cmp_ref[0, :, HD_A * g:HD_A * (g + 1)]
        vc = cmp_ref[0, :, LANE + HD_A * g:LANE + HD_A * (g + 1)]
        s = _dot_nt(q2, kc).reshape(R_A, TQ, nch) + dcmp_ref[R_A * g:R_A * (g + 1)]
        p_c = _softmax_rows(s.reshape(R_A * TQ, nch))
        p_cb = p_c.astype(BF16)
        o_cmp = _dot(p_cb, vc)
        imp_t = jnp.zeros((ns, TQ), F32)
        for r in range(R_A):
            imp_t = imp_t + _dot_nt(movt_ref[...], p_cb[r * TQ:(r + 1) * TQ])
        jj = lax.broadcasted_iota(jnp.int32, (ns, TQ), 0)
        cc = (i * TQ + lax.broadcasted_iota(jnp.int32, (ns, TQ), 1)) // SEL_BLOCK
        forced = (jj == 0) | (jj == cc) | (jj == cc - 1)
        score = jnp.where(forced, FORCE_SCORE, jnp.where(jj <= cc, imp_t, -jnp.inf))
        rank = _rank(score, ns, 0)
        unsel_t = jnp.where((rank < min(N_SEL, ns)) & (jj <= cc), 0.0, 1.0)
        unsel_main = jnp.transpose(jnp.where(jj < blocks_per_tile * (i - 1), unsel_t, 1.0)).astype(BF16)
        unsel_tail = jnp.transpose(unsel_t).astype(BF16)

        q2s.append(q2)
        o_cmps.append(o_cmp)
        lhs_mains.append(jnp.concatenate([q2, jnp.concatenate([unsel_main] * R_A, axis=0)], axis=1))
        lhs_tails.append(jnp.concatenate([q2, jnp.concatenate([unsel_tail] * R_A, axis=0)], axis=1))

    def score_tile(kt, ms):
        c0 = pl.multiple_of(kt * KT, KT)
        out = []
        for g in range(G_A):
            s = _dot(lhs_mains[g], k_ext(g, c0, KT))
            sc_ref[g, :, pl.ds(c0, KT)] = s
            out.append(jnp.maximum(ms[g], jnp.max(s, axis=-1, keepdims=True)))
        return tuple(out)

    ms = lax.fori_loop(0, n_main, score_tile, tuple(jnp.full((R_A * TQ, 1), NEG, F32) for _ in range(G_A)))
    tails = []
    for g in range(G_A):
        d_here = dsel_ref[R_A * g:R_A * (g + 1)].reshape(R_A * TQ, 2 * TQ)
        s_a = _dot(lhs_tails[g], k_ext(g, c_a, TQ)) + d_here[:, 0:TQ] + jnp.where(i == 0, NEG, 0.0)
        s_b = _dot(lhs_tails[g], k_ext(g, c_b, TQ)) + d_here[:, TQ:2 * TQ]
        m = jnp.maximum(ms[g], jnp.maximum(jnp.max(s_a, axis=-1, keepdims=True),
                                           jnp.max(s_b, axis=-1, keepdims=True)))
        tails.append((s_a, s_b, m))

    def pv_tile(kt, accs):
        c0 = pl.multiple_of(kt * KT, KT)
        out = []
        for g in range(G_A):
            p = jnp.exp(sc_ref[g, :, pl.ds(c0, KT)] - tails[g][2]).astype(BF16)
            out.append(accs[g] + _dot_nt(p, v_ext(g, c0, KT)))
        return tuple(out)

    accs = lax.fori_loop(0, n_main, pv_tile,
                         tuple(jnp.zeros((R_A * TQ, HD_A + 16), F32) for _ in range(G_A)))

    for g in range(G_A):
        q2, o_cmp = q2s[g], o_cmps[g]
        s_a, s_b, m = tails[g]
        acc = (accs[g] + _dot_nt(jnp.exp(s_a - m).astype(BF16), v_ext(g, c_a, TQ))
               + _dot_nt(jnp.exp(s_b - m).astype(BF16), v_ext(g, c_b, TQ)))

        s_w = []
        for w in range(n_win_tiles):
            kt = i - (n_win_tiles - 1) + w
            c0 = pl.multiple_of(jnp.maximum(kt, 0) * TQ, TQ)
            kw = win_ref[0, HD_A * g:HD_A * (g + 1), pl.ds(c0, TQ)]
            bw = bwin_ref[R_A * g:R_A * (g + 1), :, w * TQ:(w + 1) * TQ].reshape(R_A * TQ, TQ)
            s_w.append(_dot(q2, kw) + bw + jnp.where(kt < 0, NEG, 0.0))
        s_w = jnp.concatenate(s_w, axis=1)
        p_w = jnp.exp(s_w - jnp.max(s_w, axis=-1, keepdims=True)).astype(BF16)
        acc_w = jnp.zeros((R_A * TQ, HD_A + 16), F32)
        for w in range(n_win_tiles):
            kt = i - (n_win_tiles - 1) + w
            c0 = pl.multiple_of(jnp.maximum(kt, 0) * TQ, TQ)
            vw = jnp.concatenate([win_ref[0, LANE + HD_A * g:LANE + HD_A * (g + 1), pl.ds(c0, TQ)],
                                  ones_rows[:, 0:TQ]], axis=0)
            acc_w = acc_w + _dot_nt(p_w[:, w * TQ:(w + 1) * TQ], vw)

        for r in range(R_A):
            h = R_A * g + r
            rows = slice(r * TQ, (r + 1) * TQ)
            c_sel = gates[:, 3 * h + 1:3 * h + 2] / acc[rows, HD_A:HD_A + 1]
            c_win = gates[:, 3 * h + 2:3 * h + 3] / acc_w[rows, HD_A:HD_A + 1]
            o_ref[:, HD_A * h:HD_A * (h + 1)] = (gates[:, 3 * h:3 * h + 1] * o_cmp[rows]
                                                  + c_sel * acc[rows, 0:HD_A] + c_win * acc_w[rows, 0:HD_A])


def _prompt_nsa(rel_bias, qa, kvb, cmpk, gates, movt, emat):
    nb, _, t = kvb.shape
    ni = t // TQ
    nch = cmpk.shape[1]
    n_win_tiles = WINDOW // TQ + 1

    def full(a):
        return pl.BlockSpec(a.shape, lambda i, b, _n=a.ndim: (0,) * _n)

    return pl.pallas_call(
        _prompt_nsa_kernel,
        grid=(ni, nb),
        in_specs=[pl.BlockSpec(memory_space=pltpu.SMEM),
                  pl.BlockSpec((TQ, W_A), lambda i, b: (b * ni + i, 0)),
                  pl.BlockSpec((1, 2 * LANE, t), lambda i, b: (b, 1, 0)),
                  pl.BlockSpec((1, 2 * LANE, t), lambda i, b: (b, 2, 0)),
                  pl.BlockSpec((1, nch, 2 * LANE), lambda i, b: (b, 0, 0)),
                  pl.BlockSpec((TQ, LANE), lambda i, b: (b * ni + i, 0)),
                  full(movt), full(emat)],
        out_specs=pl.BlockSpec((TQ, W_A), lambda i, b: (b * ni + i, 0)),
        out_shape=jax.ShapeDtypeStruct((nb * t, W_A), F32),
        scratch_shapes=[pltpu.VMEM((H_A, TQ, n_win_tiles * TQ), F32),
                        pltpu.VMEM((H_A, TQ, 2 * TQ), F32),
                        pltpu.VMEM((H_A, TQ, 2 * nch), F32),
                        pltpu.VMEM((H_A, TQ, nch), F32),
                        pltpu.VMEM((G_A, R_A * TQ, t), F32)],
        compiler_params=pltpu.CompilerParams(vmem_limit_bytes=VMEM_LIMIT,
                                             dimension_semantics=("arbitrary", "arbitrary")),
        name="prompt_nsa",
    )(rel_bias, qa, kvb, kvb, cmpk, gates, movt, emat)


def _prompt_mla_kernel(qm_ref, mla_ref, o_ref, sc_ref, acc_ref):
    i = pl.program_id(1)
    nrow = H_B * TQM
    n_full = (i * TQM) // KT
    q2 = jnp.concatenate([qm_ref[:, 2 * LANE * h:2 * LANE * (h + 1)] for h in range(H_B)], axis=0)

    def score_tile(kt, m):
        c0 = pl.multiple_of(kt * KT, KT)
        s = _dot(q2, mla_ref[0, :, pl.ds(c0, KT)])
        sc_ref[:, pl.ds(c0, KT)] = s
        return jnp.maximum(m, jnp.max(s, axis=-1, keepdims=True))

    m = lax.fori_loop(0, n_full, score_tile, jnp.full((nrow, 1), NEG, F32))
    c_d = pl.multiple_of(n_full * KT, KT)
    qpos = i * TQM + lax.broadcasted_iota(jnp.int32, (H_B, TQM, KT), 1).reshape(nrow, KT)
    kpos = c_d + lax.broadcasted_iota(jnp.int32, (nrow, KT), 1)
    s_d = jnp.where(kpos <= qpos, _dot(q2, mla_ref[0, :, pl.ds(c_d, KT)]), NEG)
    m = jnp.maximum(m, jnp.max(s_d, axis=-1, keepdims=True))
    acc_ref[...] = _dot_nt(jnp.exp(s_d - m).astype(BF16), mla_ref[0, :, pl.ds(c_d, KT)])

    def pv_tile(kt, carry):
        c0 = pl.multiple_of(kt * KT, KT)
        p = jnp.exp(sc_ref[:, pl.ds(c0, KT)] - m).astype(BF16)
        acc_ref[...] += _dot_nt(p, mla_ref[0, :, pl.ds(c0, KT)])
        return carry

    lax.fori_loop(0, n_full, pv_tile, 0)
    acc = acc_ref[...]
    o = acc[:, 0:KV_LORA] / acc[:, KV_LORA + ROPE_DIM:KV_LORA + ROPE_DIM + 1]
    for h in range(H_B):
        o_ref[:, KV_LORA * h:KV_LORA * (h + 1)] = o[h * TQM:(h + 1) * TQM]


def _prompt_mla(qm, mlab):
    nb, rows, t = mlab.shape
    ni = t // TQM
    return pl.pallas_call(
        _prompt_mla_kernel,
        grid=(nb, ni),
        in_specs=[pl.BlockSpec((TQM, H_B * 2 * LANE), lambda b, i: (b * ni + i, 0)),
                  pl.BlockSpec((1, rows, t), lambda b, i: (b, 0, 0))],
        out_specs=pl.BlockSpec((TQM, H_B * KV_LORA), lambda b, i: (b * ni + i, 0)),
        out_shape=jax.ShapeDtypeStruct((nb * t, H_B * KV_LORA), F32),
        scratch_shapes=[pltpu.VMEM((H_B * TQM, t), F32), pltpu.VMEM((H_B * TQM, rows), F32)],
        compiler_params=pltpu.CompilerParams(vmem_limit_bytes=VMEM_LIMIT),
        name="prompt_mla",
    )(qm, mlab)


def _merge_kernel(x_ref, gate_ref, oa_ref, ol_ref, sza_ref, szb_ref, wuv_ref, wout_ref, gf_ref, y_ref):
    ya = oa_ref[...] * sza_ref[...]
    ol = ol_ref[...]
    szb = szb_ref[...]
    yb = [_dot(ol[:, KV_LORA * h:KV_LORA * (h + 1)].astype(BF16), wuv_ref[h]) * szb[:, V_HEAD * h:V_HEAD * (h + 1)]
          for h in range(H_B)]
    yab = jnp.concatenate([ya] + yb, axis=1).astype(BF16)
    xn = x_ref[...] + gate_ref[0] * _dot(yab, wout_ref[...])
    y_ref[...] = xn * lax.rsqrt(jnp.mean(xn * xn, axis=-1, keepdims=True) + EPS) * gf_ref[...]


def _merge(xf, gate3, per_row_mod, tb, oa, ol, sza, szb, wuv, wout, gfin):
    n, d = xf.shape
    tm = _row_tile(tb)
    tiles_b = tb // tm
    if per_row_mod:
        mod_spec = pl.BlockSpec((1, tm, d), lambda r: (r, 0, 0))
    else:
        mod_spec = pl.BlockSpec((1, 1, d), lambda r: (r // tiles_b, 0, 0))

    def full(a):
        return pl.BlockSpec(a.shape, lambda r, _n=a.ndim: (0,) * _n)

    def tmaj(cols):
        return pl.BlockSpec((tm, cols), lambda r: (r, 0))

    return pl.pallas_call(
        _merge_kernel,
        grid=(n // tm,),
        in_specs=[tmaj(d), mod_spec, tmaj(W_A), tmaj(W_B), tmaj(W_A), tmaj(W_B), full(wuv), full(wout), full(gfin)],
        out_specs=tmaj(d),
        out_shape=jax.ShapeDtypeStruct((n, d), F32),
        compiler_params=pltpu.CompilerParams(vmem_limit_bytes=VMEM_LIMIT),
        name="out_merge",
    )(xf, gate3, oa, ol, sza, szb, wuv, wout, gfin)


def _row_head_bias(bucket, tab_ref, g, shift):
    ts = bucket.shape[0] // R_A
    row_r = lax.broadcasted_iota(jnp.int32, bucket.shape, 0) // ts
    out = jnp.zeros(bucket.shape, F32)
    for r in range(R_A):
        h = R_A * g + r
        val = _bias_lookup(bucket, tab_ref, h)
        if shift:
            val = val - tab_ref[N_BUCKETS - 1, h]
        out = jnp.where(row_r == r, val, out)
    return out


def _gather_pages(pt_ref, cache_hbm, buf_ref, sem_ref, n_pages):
    b = pl.program_id(0)

    def copy(seq, k, slot):
        return pltpu.make_async_copy(cache_hbm.at[pt_ref[seq, k]], buf_ref.at[slot, k], sem_ref.at[slot])

    @pl.when(b == 0)
    def _():
        for k in range(n_pages):
            copy(0, k, 0).start()

    @pl.when(b + 1 < pl.num_programs(0))
    def _():
        for k in range(n_pages):
            copy(b + 1, k, (b + 1) % 2).start()

    slot = b % 2
    for k in range(n_pages):
        copy(b, k, slot).wait()
    return slot


def _own_lanes(x, b, ts):
    per_block = LANE // ts
    shift = (LANE - ts * (b % per_block)) % LANE
    lane = lax.broadcasted_iota(jnp.int32, x.shape, 1)
    return jnp.where(lane < ts, pltpu.roll(x, shift, 1), 0.0)


def _decode_nsa_kernel(pt_ref, tab_ref, cache_hbm, qa_ref, nsel_ref, nwin_ref, win_ref, gsel_ref, lperm_ref,
                       wc_ref, pe_ref, wflat_ref, mov_ref, o_ref, wout_ref, a_ref, stash_ref, bc_ref, ds_ref,
                       dn_ref, bw_ref, bwn_ref, buf_ref, sem_ref, *, ts, past):
    b = pl.program_id(0)
    n_pages = buf_ref.shape[1]
    slot = _gather_pages(pt_ref, cache_hbm, buf_ref, sem_ref, n_pages)
    nch = past // CMP_STRIDE
    nc = nch - 1
    nrow = R_A * ts
    n_cmp_rows = 2 * G_A * HD_A
    wb = win_ref.shape[2]

    for pair in range(n_pages // 2):
        pa = buf_ref[slot, 2 * pair]
        pb = buf_ref[slot, 2 * pair + 1]
        both = jnp.concatenate([pa[0:n_cmp_rows], pb[0:n_cmp_rows]], axis=1).astype(BF16)
        tb = _chunk_rows(lperm_ref, both)
        for j in range(CMP_STRIDE):
            a_ref[j, pair * 16:(pair + 1) * 16, :] = tb[16 * j:16 * (j + 1)]
        stash_ref[:, pair * 256:(pair + 1) * 256] = jnp.concatenate(
            [pa[n_cmp_rows:2 * n_cmp_rows], pb[n_cmp_rows:2 * n_cmp_rows]], axis=1).astype(BF16)

    trow = lax.broadcasted_iota(jnp.int32, (nrow, 1), 0) % ts

    @pl.when(b == 0)
    def _():
        n = lax.broadcasted_iota(jnp.int32, (nrow, nch), 1)
        dist = past + trow - (n * CMP_STRIDE + CMP_BLOCK - 1)
        u = lax.broadcasted_iota(jnp.int32, (nrow, LANE), 1)
        dist_s = LANE + trow - u
        dist_n = trow - u
        kl = lax.broadcasted_iota(jnp.int32, (nrow, wb), 1)
        dist_w = wb + trow - kl
        ok_n = (dist_n >= 0) & (u < ts)
        for g in range(G_A):
            bc_ref[g] = jnp.where((dist >= 0) & (n < nc), _row_head_bias(_rel_bucket(dist), tab_ref, g, False), NEG)
            ds_ref[g] = _row_head_bias(_rel_bucket(dist_s), tab_ref, g, True)
            dn_ref[g] = jnp.where(ok_n, _row_head_bias(_rel_bucket(dist_n), tab_ref, g, True), NEG)
            bw_ref[g] = jnp.where(dist_w < WINDOW, _row_head_bias(_rel_bucket(dist_w), tab_ref, g, False), NEG)
            bwn_ref[g] = jnp.where(ok_n, _row_head_bias(_rel_bucket(dist_n), tab_ref, g, False), NEG)

    cmpk = _compress_chunks(a_ref, wc_ref, pe_ref, wflat_ref, nch).astype(BF16)
    new_sel = _own_lanes(nsel_ref[0], b, ts)
    new_win = _own_lanes(nwin_ref[0], b, ts)
    nsp = mov_ref.shape[1]
    nbp = past // SEL_BLOCK
    jj = lax.broadcasted_iota(jnp.int32, (nrow, nsp), 1)
    cc = (past + trow) // SEL_BLOCK
    o_cmps, tots = [], []
    for g in range(G_A):
        q2 = qa_ref[0, g]
        s_c = _dot_nt(q2, cmpk[:, HD_A * g:HD_A * (g + 1)]) + bc_ref[g]
        p_c = _softmax_rows(s_c).astype(BF16)
        o_cmps.append(_dot(p_c, cmpk[:, LANE + HD_A * g:LANE + HD_A * (g + 1)]))
        imp = _dot(p_c, mov_ref[...])
        tot = imp
        for r in range(1, R_A):
            tot = tot + pltpu.roll(imp, r * ts, 0)
        tots.append(tot)
    half = nrow // 2
    first = lax.broadcasted_iota(jnp.int32, (half, nsp), 0) < ts
    jj8, cc8 = jj[0:half], cc[0:half]
    forced = (jj8 == 0) | (jj8 == cc8) | (jj8 == cc8 - 1)
    mixed = jnp.where(first, tots[0][0:half], tots[1][0:half])
    score = jnp.where(forced, FORCE_SCORE, jnp.where(jj8 <= cc8, mixed, -jnp.inf))
    rank = _rank(score, nbp + 1, 1)
    rank_sw = pltpu.roll(rank, ts, 0)
    allowed = jj8 <= cc8
    sel8 = [((jnp.where(first, rank, rank_sw) < N_SEL) & allowed).astype(F32)[:, 0:nbp],
            ((jnp.where(first, rank_sw, rank) < N_SEL) & allowed).astype(F32)[:, 0:nbp]]
    low = lax.broadcasted_iota(jnp.int32, (half, LANE), 1) < SEL_BLOCK
    madds = []
    for g in range(G_A):
        flags = [jnp.where(low, sel8[g][:, 2 * c:2 * c + 1], sel8[g][:, 2 * c + 1:2 * c + 2])
                 for c in range(nbp // 2)]
        m8 = (jnp.concatenate(flags, axis=1) - 1.0) * (-NEG)
        madds.append(jnp.concatenate([m8, m8], axis=0))
    for g in range(G_A):
        q2 = qa_ref[0, g]
        o_cmp = o_cmps[g]
        s_p = _dot(q2, stash_ref[HD_A * g:HD_A * (g + 1), :]) + madds[g]
        s_last = s_p[:, past - LANE:past] + ds_ref[g]
        k_new = new_sel[HD_A * g:HD_A * (g + 1)].astype(BF16)
        v_new = new_sel[LANE + HD_A * g:LANE + HD_A * (g + 1)].astype(BF16)
        s_n = _dot(q2, k_new) + dn_ref[g]
        s_all = jnp.concatenate([s_p[:, 0:past - LANE], s_last, s_n], axis=1)
        m = jnp.max(s_all, axis=-1, keepdims=True)
        p = jnp.exp(s_all - m)
        l = jnp.sum(p, axis=-1, keepdims=True)
        pb16 = p.astype(BF16)
        o_sel = (_dot_nt(pb16[:, 0:past], stash_ref[LANE + HD_A * g:LANE + HD_A * (g + 1), :])
                 + _dot_nt(pb16[:, past:past + LANE], v_new)) / l
        kw = win_ref[0, HD_A * g:HD_A * (g + 1), :].astype(BF16)
        vw = win_ref[0, LANE + HD_A * g:LANE + HD_A * (g + 1), :].astype(BF16)
        kw_new = new_win[HD_A * g:HD_A * (g + 1)].astype(BF16)
        vw_new = new_win[LANE + HD_A * g:LANE + HD_A * (g + 1)].astype(BF16)
        s_w = jnp.concatenate([_dot(q2, kw) + bw_ref[g], _dot(q2, kw_new) + bwn_ref[g]], axis=1)
        p_w = _softmax_rows(s_w).astype(BF16)
        o_win = _dot_nt(p_w[:, 0:wb], vw) + _dot_nt(p_w[:, wb:wb + LANE], vw_new)
        gs = gsel_ref[0, g]
        o_ref[0, g] = gs[:, 0:1] * o_cmp + gs[:, 1:2] * o_sel + gs[:, 2:3] * o_win

    win = win_ref[0]
    rolled = pltpu.roll(win, wb - ts, 1)
    new_w = pltpu.roll(new_win, LANE - ts, 1)
    lane = lax.broadcasted_iota(jnp.int32, (win.shape[0], LANE), 1)
    last = jnp.where(lane >= LANE - ts, new_w, rolled[:, wb - LANE:wb])
    wout_ref[0] = jnp.concatenate([rolled[:, 0:wb - LANE], last], axis=1)


def _decode_nsa(page_table, rel_bias, cache_t, qa_s, nsat_s, wint_s, win_t, gsel, cw, mov, ts, past):
    bd, n_pages = page_table.shape
    lperm, wc, pe, wflat = cw
    nrow = R_A * ts
    nch = past // CMP_STRIDE
    wb = win_t.shape[2]
    feat = cache_t.shape[1]
    per_block = LANE // ts

    def full(a):
        return pl.BlockSpec(a.shape, lambda b, pt, _n=a.ndim: (0,) * _n, pipeline_mode=pl.Buffered(1))

    in_specs = [pl.BlockSpec(memory_space=pltpu.SMEM),
                pl.BlockSpec(memory_space=pl.ANY),
                pl.BlockSpec((1, G_A, nrow, HD_A), lambda b, pt: (b, 0, 0, 0)),
                pl.BlockSpec((1, 2 * LANE, LANE), lambda b, pt: (0, 1, b // per_block)),
                pl.BlockSpec((1, 2 * LANE, LANE), lambda b, pt: (0, 0, b // per_block)),
                pl.BlockSpec((1, 2 * LANE, wb), lambda b, pt: (b, 0, 0)),
                pl.BlockSpec((1, G_A, nrow, LANE), lambda b, pt: (b, 0, 0, 0)),
                full(lperm), full(wc), full(pe), full(wflat), full(mov)]
    grid_spec = pltpu.PrefetchScalarGridSpec(
        num_scalar_prefetch=1,
        grid=(bd,),
        in_specs=in_specs,
        out_specs=[pl.BlockSpec((1, G_A, nrow, HD_A), lambda b, pt: (b, 0, 0, 0)),
                   pl.BlockSpec((1, 2 * LANE, wb), lambda b, pt: (b, 0, 0))],
        scratch_shapes=[pltpu.VMEM((CMP_STRIDE, nch, 2 * LANE), BF16),
                        pltpu.VMEM((2 * LANE, past), BF16),
                        pltpu.VMEM((G_A, nrow, nch), F32),
                        pltpu.VMEM((G_A, nrow, LANE), F32),
                        pltpu.VMEM((G_A, nrow, LANE), F32),
                        pltpu.VMEM((G_A, nrow, wb), F32),
                        pltpu.VMEM((G_A, nrow, LANE), F32),
                        pltpu.VMEM((2, n_pages, feat, LANE), F32),
                        pltpu.SemaphoreType.DMA((2,))])
    return pl.pallas_call(
        functools.partial(_decode_nsa_kernel, ts=ts, past=past),
        grid_spec=grid_spec,
        out_shape=(jax.ShapeDtypeStruct((bd, G_A, nrow, HD_A), F32),
                   jax.ShapeDtypeStruct((bd, 2 * LANE, wb), F32)),
        compiler_params=pltpu.CompilerParams(vmem_limit_bytes=VMEM_LIMIT,
                                             dimension_semantics=("arbitrary",)),
        name="decode_nsa",
    )(page_table, rel_bias, cache_t, qa_s, nsat_s, wint_s, win_t, gsel, lperm, wc, pe, wflat, mov)


def _decode_mla_kernel(pt_ref, cache_hbm, qm_ref, new_ref, o_ref, buf_ref, sem_ref, *, ts, n_pages):
    slot = _gather_pages(pt_ref, cache_hbm, buf_ref, sem_ref, n_pages)
    nrow = H_B * ts
    q2 = qm_ref[0]
    q_lat, q_rope = q2[:, 0:KV_LORA], q2[:, KV_LORA:KV_LORA + ROPE_DIM]
    feat = KV_LORA + ROPE_DIM

    def scores_of(kt):
        return _dot(q_lat, kt[0:KV_LORA]) + _dot(q_rope, kt[KV_LORA:feat])

    keys, scores = [], []
    for grp in range(n_pages // MLA_PAGE_GROUP):
        kt = jnp.concatenate([buf_ref[slot, grp * MLA_PAGE_GROUP + k].astype(BF16)
                              for k in range(MLA_PAGE_GROUP)], axis=1)
        keys.append(kt)
        scores.append(scores_of(kt))
    kn = _own_lanes(new_ref[0], pl.program_id(0), ts).astype(BF16)
    trow = lax.broadcasted_iota(jnp.int32, (nrow, LANE), 0) % ts
    u = lax.broadcasted_iota(jnp.int32, (nrow, LANE), 1)
    scores.append(scores_of(kn) + jnp.where((u <= trow) & (u < ts), 0.0, NEG))
    s_all = jnp.concatenate(scores, axis=1)
    m = jnp.max(s_all, axis=-1, keepdims=True)
    p = jnp.exp(s_all - m)
    l = jnp.sum(p, axis=-1, keepdims=True)
    pb = p.astype(BF16)
    gw = MLA_PAGE_GROUP * LANE
    acc = _dot_nt(pb[:, len(keys) * gw:len(keys) * gw + LANE], kn[0:KV_LORA])
    for grp, kt in enumerate(keys):
        acc = acc + _dot_nt(pb[:, grp * gw:(grp + 1) * gw], kt[0:KV_LORA])
    o_ref[0] = acc / l


def _decode_mla(page_table, cache_t, qm_s, new_mla, ts):
    bd, n_pages = page_table.shape
    nrow = H_B * ts
    feat = cache_t.shape[1]

    grid_spec = pltpu.PrefetchScalarGridSpec(
        num_scalar_prefetch=1,
        grid=(bd,),
        in_specs=[pl.BlockSpec(memory_space=pl.ANY),
                  pl.BlockSpec((1, nrow, 2 * LANE), lambda b, pt: (b, 0, 0)),
                  pl.BlockSpec((1, feat, LANE), lambda b, pt: (0, 0, b // (LANE // ts)))],
        out_specs=pl.BlockSpec((1, nrow, KV_LORA), lambda b, pt: (b, 0, 0)),
        scratch_shapes=[pltpu.VMEM((2, n_pages, feat, LANE), F32), pltpu.SemaphoreType.DMA((2,))])
    return pl.pallas_call(
        functools.partial(_decode_mla_kernel, ts=ts, n_pages=n_pages),
        grid_spec=grid_spec,
        out_shape=jax.ShapeDtypeStruct((bd, nrow, KV_LORA), F32),
        compiler_params=pltpu.CompilerParams(vmem_limit_bytes=VMEM_LIMIT,
                                             dimension_semantics=("arbitrary",)),
        name="decode_mla",
    )(page_table, cache_t, qm_s, new_mla)


def _rope_tables(pos):
    half = ROPE_DIM // 2
    inv = ROPE_BASE ** (-jnp.arange(half, dtype=F32) / half)
    ang = pos.astype(F32)[:, None] * inv[None, :]
    cos, sin = jnp.cos(ang), jnp.sin(ang)
    z = jnp.zeros((pos.shape[0], LANE - ROPE_DIM), F32)
    cos_tm = jnp.concatenate([cos, cos, z], axis=1)
    sin_tm = jnp.concatenate([sin, sin, z], axis=1)
    return cos_tm, sin_tm, cos.T, sin.T


def _overlap_t(n_cmp_rows, n_sel):
    cs = jnp.arange(n_cmp_rows)[None, :] * CMP_STRIDE
    ss = jnp.arange(n_sel)[:, None] * SEL_BLOCK
    ov = jnp.minimum(cs + CMP_BLOCK, ss + SEL_BLOCK) - jnp.maximum(cs, ss)
    ov = jnp.maximum(ov, 0).astype(F32) / CMP_BLOCK
    return jnp.where(jnp.arange(n_cmp_rows)[None, :] < n_cmp_rows - 1, ov, 0.0)


def _compress_weights(w_k, w_v, pe_k, pe_v):
    eye = jnp.eye(G_A, dtype=F32)

    def pair_blocks(w):
        w6 = w.reshape(2, CMP_STRIDE // 2, 2, HD_A, HD_A)
        blk = jnp.einsum('hpjde,gk->pjgdhke', w6, eye)
        return blk.reshape(CMP_STRIDE // 2, 2 * LANE, 2 * LANE)

    wc = jnp.stack([pair_blocks(w_k), pair_blocks(w_v)]).astype(BF16)
    pe = jnp.stack([jnp.tile(pe_k.reshape(1, -1), (8, 1)), jnp.tile(pe_v.reshape(1, -1), (8, 1))]).astype(BF16)
    wflat = jnp.stack([jnp.tile(w_k.reshape(-1, HD_A), (1, G_A)),
                       jnp.tile(w_v.reshape(-1, HD_A), (1, G_A))]).astype(BF16)
    r = jnp.arange(256)
    lperm = (jnp.arange(256)[None, :] == ((r % 16) * 16 + r // 16)[:, None]).astype(BF16)
    return lperm, wc, pe, wflat


def kernel(x_prompt, x_sample, cache_nsa, cache_mla, state_win, page_table, c_prompt, c_sample,
           w_ada, b_ada, g_norm, w_in, w_cmp_k, w_cmp_v, pe_cmp_k, pe_cmp_v, rel_bias,
           g_q_lat, w_uq, g_kv_lat, w_uk, w_uv, w_out, g_final):
    assert w_in.shape[0] == 1, "one layer"
    nb, t, d = x_prompt.shape
    bd, ts, _ = x_sample.shape
    n_pages = page_table.shape[1]
    page = cache_nsa.shape[2]
    past = n_pages * page
    assert page == LANE and (past + ts) // CMP_STRIDE == past // CMP_STRIDE and ts <= SEL_BLOCK
    assert (R_A * ts // 2) % ts == 0 and t % TQM == 0 and LANE % ts == 0 and (bd * ts) % LANE == 0
    assert t % KT == 0 and (bd * ts) % (TM // 2) == 0 and n_pages % MLA_PAGE_GROUP == 0 and n_pages % 2 == 0
    wb = state_win.shape[2]

    wi = w_in[0]
    o1 = W_A
    o2 = o1 + 6 * G_A * HD_A
    o3 = o2 + 3 * H_A
    o4 = o3 + W_A
    o5 = o4 + Q_LORA
    o6 = o5 + KV_LORA
    o7 = o6 + ROPE_DIM
    wg = jnp.pad(wi[:, o2:o3], ((0, 0), (0, LANE - 3 * H_A)))
    wt = jnp.concatenate([wi[:, 0:o1] * (HD_A ** -0.5), wi[:, o3:o4], wi[:, o7:], wi[:, o4:o5], wg], axis=1).astype(BF16)
    wf = jnp.concatenate([wi[:, o1:o2], wi[:, o5:o6], wi[:, o6:o7]], axis=1).T.astype(BF16)
    wq = w_uq[0]
    wqn = wq[:, :, :NOPE].reshape(Q_LORA, H_B * NOPE).astype(BF16)
    wr = wq[:, :, NOPE:]
    half = ROPE_DIM // 2
    wsw = jnp.concatenate([-wr[..., half:], wr[..., :half]], axis=-1)
    padr = ((0, 0), (0, 0), (0, LANE - ROPE_DIM))
    wqr = jnp.pad(wr, padr).reshape(Q_LORA, H_B * LANE).astype(BF16)
    wqs = jnp.pad(wsw, padr).reshape(Q_LORA, H_B * LANE).astype(BF16)
    wuk = jnp.transpose(w_uk[0], (1, 2, 0)).astype(BF16)
    wuv = jnp.transpose(w_uv[0], (1, 0, 2)).astype(BF16)
    wts = (g_norm[0][None], wt, wf, g_q_lat[0][None], wqn, wqr, wqs, wuk, g_kv_lat[0][:, None])
    cw = _compress_weights(w_cmp_k[0], w_cmp_v[0], pe_cmp_k[0], pe_cmp_v[0])
    wout = w_out[0].astype(BF16)
    gfin = g_final[None]

    n_c = nb + bd
    n_cp = -(-n_c // 8) * 8
    c_all = jnp.pad(jnp.concatenate([c_prompt, c_sample], axis=0), ((0, n_cp - n_c), (0, 0)))
    mod = _modulation(c_all, w_ada[0], b_ada)
    shift, scale, gate = mod[:, 0:d], mod[:, d:2 * d], mod[:, 2 * d:3 * d]

    def prompt_mod(a):
        return a[0:nb][:, None, :]

    def sample_mod(a):
        tm_s = _row_tile(bd * ts)
        return jnp.repeat(a[nb:nb + bd], ts, axis=0).reshape(bd * ts // tm_s, tm_s, d)

    pos_p = jnp.arange(t, dtype=jnp.int32)
    xp = x_prompt.reshape(nb * t, d)
    (qa, gates, sza, szb, qm, nsat, wint, kvb, mlat, mlab) = _project(
        xp, prompt_mod(shift), prompt_mod(scale), False, nb, t, wts, *_rope_tables(pos_p))
    cmpk = _prompt_compress(kvb, cw)
    nch_p = t // CMP_STRIDE
    ns_p = t // SEL_BLOCK
    movt = _overlap_t(nch_p, ns_p).astype(BF16)
    emat = jnp.where(jnp.arange(t)[None, :] // SEL_BLOCK == jnp.arange(ns_p)[:, None], NEG, 0.0).astype(BF16)
    oa = _prompt_nsa(rel_bias, qa, kvb, cmpk, gates, movt, emat)
    ol = _prompt_mla(qm, mlab)
    y_prompt = _merge(xp, prompt_mod(gate), False, t, oa, ol, sza, szb, wuv, wout, gfin).reshape(nb, t, d)

    n_s = bd * ts
    pos_s = past + jnp.arange(ts, dtype=jnp.int32)
    cos_tm, sin_tm, cos_t, sin_t = _rope_tables(pos_s)
    xs = x_sample.reshape(n_s, d)
    (qa_s, gates_s, sza_s, szb_s, qm_s, nsat_s, wint_s, _, mlat_s, _) = _project(
        xs, sample_mod(shift), sample_mod(scale), True, 1, n_s, wts,
        jnp.tile(cos_tm, (bd, 1)), jnp.tile(sin_tm, (bd, 1)), jnp.tile(cos_t, (1, bd)), jnp.tile(sin_t, (1, bd)))
    nrow = R_A * ts
    qa_g = qa_s.reshape(bd, ts, G_A, R_A, HD_A).transpose(0, 2, 3, 1, 4).reshape(bd, G_A, nrow, HD_A)
    gsel = gates_s[:, 0:3 * H_A].reshape(bd, ts, G_A, R_A, 3).transpose(0, 2, 3, 1, 4).reshape(bd, G_A, nrow, 3)
    gsel = jnp.pad(gsel, ((0, 0), (0, 0), (0, 0), (0, LANE - 3)))
    qm_h = qm_s.reshape(bd, ts, H_B, 2 * LANE).transpose(0, 2, 1, 3).reshape(bd, H_B * ts, 2 * LANE)
    n_phys = cache_nsa.shape[1]
    cache_nsa_t = jnp.transpose(cache_nsa[0], (0, 2, 3, 4, 1)).reshape(n_phys, 4 * G_A * HD_A, page)
    cache_mla_t = jnp.transpose(cache_mla[0], (0, 2, 1))
    win_t = jnp.transpose(state_win[0], (0, 2, 3, 4, 1)).reshape(bd, 2 * G_A * HD_A, wb)
    nch_s = past // CMP_STRIDE
    nbp = past // SEL_BLOCK
    nsp = -(-(nbp + 1) // LANE) * LANE
    mov_s = jnp.pad(_overlap_t(nch_s, nbp + 1).T, ((0, 0), (0, nsp - nbp - 1))).astype(BF16)
    oa_g, win_new = _decode_nsa(page_table, rel_bias, cache_nsa_t, qa_g, nsat_s, wint_s, win_t, gsel, cw,
                                mov_s, ts, past)
    ol_h = _decode_mla(page_table, cache_mla_t, qm_h, mlat_s, ts)
    oa_s = oa_g.reshape(bd, G_A, R_A, ts, HD_A).transpose(0, 3, 1, 2, 4).reshape(n_s, W_A)
    ol_s = ol_h.reshape(bd, H_B, ts, KV_LORA).transpose(0, 2, 1, 3).reshape(n_s, H_B * KV_LORA)
    y_sample = _merge(xs, sample_mod(gate), True, n_s, oa_s, ol_s, sza_s, szb_s, wuv, wout, gfin).reshape(bd, ts, d)

    new_nsa_prompt = nsat.reshape(nb, 4, G_A, HD_A, t).transpose(0, 4, 1, 2, 3)[None]
    new_nsa_sample = nsat_s[0].reshape(4, G_A, HD_A, bd, ts).transpose(3, 4, 0, 1, 2)[None]
    new_mla_prompt = mlat.transpose(0, 2, 1)[None]
    new_mla_sample = mlat_s[0].reshape(KV_LORA + ROPE_DIM, bd, ts).transpose(1, 2, 0)[None]
    wlen = min(WINDOW, t)
    new_win_prompt = wint[:, :, t - wlen:].reshape(nb, 2, G_A, HD_A, wlen).transpose(0, 4, 1, 2, 3)[None]
    new_win_sample = win_new.reshape(bd, 2, G_A, HD_A, wb).transpose(0, 4, 1, 2, 3)[None]
    return (y_prompt, y_sample, new_nsa_prompt, new_nsa_sample, new_mla_prompt, new_mla_sample,
            new_win_prompt, new_win_sample)
```

```python
import functools
import math

import jax
import jax.numpy as jnp
from jax import lax
from jax.experimental import pallas as pl
from jax.experimental.pallas import tpu as pltpu

F32 = jnp.float32
BF16 = jnp.bfloat16

H_A, G_A, R_A, HD_A = 8, 2, 4, 64
W_A = H_A * HD_A
CMP_BLOCK, CMP_STRIDE = 32, 16
SEL_BLOCK, N_SEL = 64, 16
WINDOW = 512
FORCE_SCORE = 1e4
H_B, NOPE, ROPE_DIM, V_HEAD, Q_LORA, KV_LORA = 4, 128, 64, 128, 256, 128
W_B = H_B * V_HEAD
ROPE_BASE = 10000.0
N_BUCKETS, MAX_DIST = 32, 128
EPS = 1e-6
NEG = -1e30

TQ = 128
TQM = 256
KT = 1024
TM = 512
LANE = 128
DEC_GROUP = 16
DEC_AHEAD = 2
DEC_RING = 4
VMEM_LIMIT = 56 * 1024 * 1024


def _row_tile(rows):
    return TM if rows % TM == 0 else TM // 2


def _dot(a, b):
    return jnp.dot(a, b, preferred_element_type=F32)


def _dot_nt(a, b):
    return lax.dot_general(a, b, (((1,), (1,)), ((), ())), preferred_element_type=F32)


def _rel_bucket(dist):
    n = jnp.maximum(dist, 0)
    max_exact = N_BUCKETS // 2
    span = N_BUCKETS - max_exact
    large = jnp.full(n.shape, max_exact, jnp.int32)
    for k in range(1, span):
        thr = math.ceil(max_exact * (MAX_DIST / max_exact) ** (k / span))
        large = large + (n >= thr).astype(jnp.int32)
    return jnp.where(n < max_exact, n, large)


def _bias_lookup(bucket, tab_ref, h):
    acc = jnp.zeros(bucket.shape, F32)
    for k in range(N_BUCKETS):
        acc = jnp.where(bucket == k, tab_ref[k, h], acc)
    return acc


def _softmax_rows(s):
    m = jnp.max(s, axis=-1, keepdims=True)
    p = jnp.exp(s - m)
    l = jnp.sum(p, axis=-1, keepdims=True)
    return jnp.where(s > 0.5 * NEG, p / l, 0.0)


def _mod_kernel(c_ref, w_ref, b_ref, o_ref):
    o_ref[...] = _dot(c_ref[...].astype(BF16), w_ref[...].astype(BF16)) + b_ref[...]


def _modulation(c, w_ada, b_ada):
    n, d = c.shape
    n3 = w_ada.shape[1]
    tn = 512
    return pl.pallas_call(
        _mod_kernel,
        grid=(n3 // tn,),
        in_specs=[pl.BlockSpec((n, d), lambda j: (0, 0)),
                  pl.BlockSpec((d, tn), lambda j: (0, j)),
                  pl.BlockSpec((1, tn), lambda j: (0, j))],
        out_specs=pl.BlockSpec((n, tn), lambda j: (0, j)),
        out_shape=jax.ShapeDtypeStruct((n, n3), F32),
        name="adaln_mod",
    )(c, w_ada, b_ada)


def _proj_kernel(x_ref, shift_ref, scale_ref, gn_ref, wt_ref, wf_ref, gq_ref, wqn_ref, wqr_ref,
                 wqs_ref, wuk_ref, gkv_ref, cos_ref, sin_ref, cost_ref, sint_ref,
                 qa_ref, gates_ref, sza_ref, szb_ref, qm_ref, nsat_ref, wint_ref, kvb_ref,
                 mlat_ref, mlab_ref, *, mla_scale):
    x = x_ref[...]
    y = x * lax.rsqrt(jnp.mean(x * x, axis=-1, keepdims=True) + EPS) * gn_ref[...]
    h = y * (1.0 + scale_ref[0]) + shift_ref[0]
    hb = h.astype(BF16)

    ut = _dot(hb, wt_ref[...])
    qa_ref[...] = ut[:, 0:W_A].astype(BF16)
    za = ut[:, W_A:2 * W_A]
    zb = ut[:, 2 * W_A:3 * W_A]
    sza_ref[...] = za * jax.nn.sigmoid(za)
    szb_ref[...] = zb * jax.nn.sigmoid(zb)
    cq = ut[:, 3 * W_A:3 * W_A + Q_LORA]
    gates_ref[...] = jax.nn.sigmoid(ut[:, 3 * W_A + Q_LORA:3 * W_A + Q_LORA + LANE])

    cqn = (cq * lax.rsqrt(jnp.mean(cq * cq, axis=-1, keepdims=True) + EPS) * gq_ref[...]).astype(BF16)
    qn = _dot(cqn, wqn_ref[...])
    qr = _dot(cqn, wqr_ref[...])
    qs = _dot(cqn, wqs_ref[...])
    cos_q = jnp.concatenate([cos_ref[...]] * H_B, axis=1)
    sin_q = jnp.concatenate([sin_ref[...]] * H_B, axis=1)
    q_rope = qr * cos_q + qs * sin_q
    for hh in range(H_B):
        q_lat = _dot(qn[:, hh * NOPE:(hh + 1) * NOPE].astype(BF16), wuk_ref[hh])
        qm_ref[:, 2 * LANE * hh:2 * LANE * hh + LANE] = (q_lat * mla_scale).astype(BF16)
        qm_ref[:, 2 * LANE * hh + LANE:2 * LANE * (hh + 1)] = (
            q_rope[:, hh * LANE:(hh + 1) * LANE] * mla_scale).astype(BF16)

    uf = _dot_nt(wf_ref[...], hb)
    n_sel_rows = 4 * G_A * HD_A
    n_kv_rows = 6 * G_A * HD_A
    nsat_ref[0] = uf[0:n_sel_rows]
    wint_ref[0] = uf[n_sel_rows:n_kv_rows]
    kvb_ref[0] = uf[0:n_kv_rows].astype(BF16)
    c = uf[n_kv_rows:n_kv_rows + KV_LORA]
    cn = c * lax.rsqrt(jnp.mean(c * c, axis=0, keepdims=True) + EPS) * gkv_ref[...]
    kr = uf[n_kv_rows + KV_LORA:n_kv_rows + KV_LORA + ROPE_DIM]
    half = ROPE_DIM // 2
    x1, x2 = kr[0:half], kr[half:ROPE_DIM]
    ct, st = cost_ref[...], sint_ref[...]
    mla = jnp.concatenate([cn, x1 * ct - x2 * st, x1 * st + x2 * ct], axis=0)
    mlat_ref[0] = mla
    n_pad = 2 * LANE - KV_LORA - ROPE_DIM
    pad = jnp.where(lax.broadcasted_iota(jnp.int32, (n_pad, mla.shape[1]), 0) == 0, 1.0, 0.0)
    mlab_ref[0] = jnp.concatenate([mla, pad], axis=0).astype(BF16)


def _project(xf, shift3, scale3, per_row_mod, nb, tb, wts, cos_tm, sin_tm, cos_t, sin_t):
    n, d = xf.shape
    tm = _row_tile(tb)
    tiles_b = tb // tm
    ntab = cos_tm.shape[0] // tm
    (gn, wt, wf, gq, wqn, wqr, wqs, wuk, gkv) = wts
    if per_row_mod:
        mod_spec = pl.BlockSpec((1, tm, d), lambda r: (r, 0, 0))
    else:
        mod_spec = pl.BlockSpec((1, 1, d), lambda r: (r // tiles_b, 0, 0))

    def full(a):
        return pl.BlockSpec(a.shape, lambda r, _n=a.ndim: (0,) * _n)

    def fm(rows):
        return pl.BlockSpec((1, rows, tm), lambda r: (r // tiles_b, 0, r % tiles_b))

    def tmaj(cols):
        return pl.BlockSpec((tm, cols), lambda r: (r, 0))

    n_sel_rows = 4 * G_A * HD_A
    out_shape = (
        jax.ShapeDtypeStruct((n, W_A), BF16),
        jax.ShapeDtypeStruct((n, LANE), F32),
        jax.ShapeDtypeStruct((n, W_A), F32),
        jax.ShapeDtypeStruct((n, W_B), F32),
        jax.ShapeDtypeStruct((n, H_B * 2 * LANE), BF16),
        jax.ShapeDtypeStruct((nb, n_sel_rows, tb), F32),
        jax.ShapeDtypeStruct((nb, 2 * G_A * HD_A, tb), F32),
        jax.ShapeDtypeStruct((nb, 6 * G_A * HD_A, tb), BF16),
        jax.ShapeDtypeStruct((nb, KV_LORA + ROPE_DIM, tb), F32),
        jax.ShapeDtypeStruct((nb, 2 * LANE, tb), BF16),
    )
    out_specs = (tmaj(W_A), tmaj(LANE), tmaj(W_A), tmaj(W_B), tmaj(H_B * 2 * LANE),
                 fm(n_sel_rows), fm(2 * G_A * HD_A), fm(6 * G_A * HD_A), fm(KV_LORA + ROPE_DIM), fm(2 * LANE))
    in_specs = [tmaj(d), mod_spec, mod_spec, full(gn), full(wt), full(wf), full(gq), full(wqn),
                full(wqr), full(wqs), full(wuk), full(gkv),
                pl.BlockSpec((tm, LANE), lambda r: (r % ntab, 0)),
                pl.BlockSpec((tm, LANE), lambda r: (r % ntab, 0)),
                pl.BlockSpec((ROPE_DIM // 2, tm), lambda r: (0, r % ntab)),
                pl.BlockSpec((ROPE_DIM // 2, tm), lambda r: (0, r % ntab))]
    return pl.pallas_call(
        functools.partial(_proj_kernel, mla_scale=(NOPE + ROPE_DIM) ** -0.5),
        grid=(n // tm,),
        in_specs=in_specs,
        out_specs=out_specs,
        out_shape=out_shape,
        compiler_params=pltpu.CompilerParams(vmem_limit_bytes=VMEM_LIMIT),
        name="in_proj",
    )(xf, shift3, scale3, gn, wt, wf, gq, wqn, wqr, wqs, wuk, gkv, cos_tm, sin_tm, cos_t, sin_t)


def _chunk_rows(lperm_ref, feats_tok):
    return _dot_nt(lperm_ref[...], feats_tok).astype(BF16)


def _compress_chunks(a_ref, wc_ref, pe_ref, wflat_ref, nch):
    outs = []
    for kind in range(2):
        acc = jnp.zeros((nch, 2 * LANE), F32)
        for p in range(CMP_STRIDE // 2):
            lhs = jnp.concatenate([a_ref[2 * p, :, kind * LANE:(kind + 1) * LANE],
                                   a_ref[2 * p + 1, :, kind * LANE:(kind + 1) * LANE]], axis=1)
            acc = acc + _dot(lhs, wc_ref[kind, p])
        first, second = acc[:, 0:LANE], acc[:, LANE:2 * LANE]
        cb = _dot(pe_ref[kind], wflat_ref[kind])[0:1]
        outs.append(first + pltpu.roll(second, nch - 1, 0) + cb)
    return jnp.concatenate(outs, axis=1)


def _prompt_compress_kernel(kv_ref, lperm_ref, wc_ref, pe_ref, wflat_ref, o_ref, a_ref):
    t = kv_ref.shape[2]
    nch = t // CMP_STRIDE
    for tix in range(t // 256):
        tb = _chunk_rows(lperm_ref, kv_ref[0, :, tix * 256:(tix + 1) * 256])
        for j in range(CMP_STRIDE):
            a_ref[j, tix * 16:(tix + 1) * 16, :] = tb[16 * j:16 * (j + 1)]
    o_ref[0] = _compress_chunks(a_ref, wc_ref, pe_ref, wflat_ref, nch).astype(BF16)


def _prompt_compress(kvb, cw):
    nb, _, t = kvb.shape
    nch = t // CMP_STRIDE
    lperm, wc, pe, wflat = cw

    def full(a):
        return pl.BlockSpec(a.shape, lambda b, _n=a.ndim: (0,) * _n)

    return pl.pallas_call(
        _prompt_compress_kernel,
        grid=(nb,),
        in_specs=[pl.BlockSpec((1, 2 * LANE, t), lambda b: (b, 0, 0)), full(lperm), full(wc), full(pe), full(wflat)],
        out_specs=pl.BlockSpec((1, nch, 2 * LANE), lambda b: (b, 0, 0)),
        out_shape=jax.ShapeDtypeStruct((nb, nch, 2 * LANE), BF16),
        scratch_shapes=[pltpu.VMEM((CMP_STRIDE, nch, 2 * LANE), BF16)],
        compiler_params=pltpu.CompilerParams(vmem_limit_bytes=VMEM_LIMIT),
        name="prompt_compress",
    )(kvb, lperm, wc, pe, wflat)


def _rank(score, n_valid, axis):
    size = 8 if axis == 0 else LANE
    ntile = score.shape[axis] // size

    def tile(a, k):
        return a[k * size:(k + 1) * size, :] if axis == 0 else a[:, k * size:(k + 1) * size]

    tiles = [tile(score, k) for k in range(ntile)]
    ranks = [jnp.zeros(tiles[0].shape, F32) for _ in range(ntile)]
    pos = lax.broadcasted_iota(jnp.int32, tiles[0].shape, axis)
    for j2 in range(n_valid):
        kt, off = divmod(j2, size)
        cand = tiles[kt][off:off + 1, :] if axis == 0 else tiles[kt][:, off:off + 1]
        for k in range(ntile):
            if k > kt:
                before = cand >= tiles[k]
            elif k < kt:
                before = cand > tiles[k]
            else:
                before = (cand > tiles[k]) | ((cand == tiles[k]) & (pos > off))
            ranks[k] = ranks[k] + jnp.where(before, 1.0, 0.0)
    return jnp.concatenate(ranks, axis=axis)


def _prompt_nsa_kernel(tab_ref, qa_ref, sel_ref, win_ref, cmp_ref, gates_ref, movt_ref, eneg_ref,
                       o_ref, bwin_ref, dsel_ref, gcmp_ref, dcmp_ref, sc_ref):
    i = pl.program_id(0)
    b = pl.program_id(1)
    t = sel_ref.shape[2]
    nch = cmp_ref.shape[1]
    ns = movt_ref.shape[0]
    n_win_tiles = WINDOW // TQ + 1
    chunks_per_tile = TQ // CMP_STRIDE
    m_off = chunks_per_tile * (t // TQ - 1)

    @pl.when((i == 0) & (b == 0))
    def _():
        tl = lax.broadcasted_iota(jnp.int32, (TQ, n_win_tiles * TQ), 0)
        kl = lax.broadcasted_iota(jnp.int32, (TQ, n_win_tiles * TQ), 1)
        dist = tl + WINDOW - kl
        bucket = _rel_bucket(dist)
        ok = (dist >= 0) & (dist < WINDOW)
        tl2 = lax.broadcasted_iota(jnp.int32, (TQ, 2 * TQ), 0)
        kl2 = lax.broadcasted_iota(jnp.int32, (TQ, 2 * TQ), 1)
        dist2 = tl2 + TQ - kl2
        bucket2 = _rel_bucket(dist2)
        for h in range(H_A):
            bwin_ref[h] = jnp.where(ok, _bias_lookup(bucket, tab_ref, h), NEG)
            dsel_ref[h] = jnp.where(dist2 >= 0,
                                    _bias_lookup(bucket2, tab_ref, h) - tab_ref[N_BUCKETS - 1, h], NEG)
        tl3 = lax.broadcasted_iota(jnp.int32, (TQ, 2 * nch), 0)
        m3 = lax.broadcasted_iota(jnp.int32, (TQ, 2 * nch), 1)
        dist3 = tl3 - (m3 - m_off) * CMP_STRIDE - (CMP_BLOCK - 1)
        bucket3 = _rel_bucket(dist3)
        for h in range(H_A):
            gcmp_ref[h] = jnp.where(dist3 >= 0,
                                    _bias_lookup(bucket3, tab_ref, h) - tab_ref[N_BUCKETS - 1, h], NEG)

    @pl.when(b == 0)
    def _():
        shift = (2 * nch - (m_off - chunks_per_tile * i)) % (2 * nch)
        for h in range(H_A):
            dcmp_ref[h] = pltpu.roll(gcmp_ref[h], shift, 1)[:, 0:nch]

    i_prev = jnp.maximum(i - 1, 0)
    n_main = (i_prev * TQ + KT - 1) // KT
    c_a = pl.multiple_of(i_prev * TQ, TQ)
    c_b = pl.multiple_of(i * TQ, TQ)
    blocks_per_tile = TQ // SEL_BLOCK
    ones_rows = jnp.ones((16, KT), BF16)
    gates = gates_ref[...]

    def k_ext(g, c0, width):
        return jnp.concatenate([sel_ref[0, HD_A * g:HD_A * (g + 1), pl.ds(c0, width)],
                                eneg_ref[:, pl.ds(c0, width)]], axis=0)

    def v_ext(g, c0, width):
        return jnp.concatenate([sel_ref[0, LANE + HD_A * g:LANE + HD_A * (g + 1), pl.ds(c0, width)],
                                ones_rows[:, 0:width]], axis=0)

    q2s, o_cmps, lhs_mains, lhs_tails = [], [], [], []
    for g in range(G_A):
        q2 = jnp.concatenate([qa_ref[:, HD_A * (R_A * g + r):HD_A * (R_A * g + r + 1)] for r in range(R_A)],
                             axis=0)
        kc = cmp_ref[0, :, HD_A * g:HD_A * (g + 1)]
        vc = cmp_ref[0, :, LANE + HD_A * g:LANE + HD_A * (g + 1)]
        s = _dot_nt(q2, kc).reshape(R_A, TQ, nch) + dcmp_ref[R_A * g:R_A * (g + 1)]
        p_c = _softmax_rows(s.reshape(R_A * TQ, nch))
        p_cb = p_c.astype(BF16)
        o_cmp = _dot(p_cb, vc)
        imp_t = jnp.zeros((ns, TQ), F32)
        for r in range(R_A):
            imp_t = imp_t + _dot_nt(movt_ref[...], p_cb[r * TQ:(r + 1) * TQ])
        jj = lax.broadcasted_iota(jnp.int32, (ns, TQ), 0)
        cc = (i * TQ + lax.broadcasted_iota(jnp.int32, (ns, TQ), 1)) // SEL_BLOCK
        forced = (jj == 0) | (jj == cc) | (jj == cc - 1)
        score = jnp.where(forced, FORCE_SCORE, jnp.where(jj <= cc, imp_t, -jnp.inf))
        rank = _rank(score, ns, 0)
        unsel_t = jnp.where((rank < min(N_SEL, ns)) & (jj <= cc), 0.0, 1.0)
        unsel_main = jnp.transpose(jnp.where(jj < blocks_per_tile * (i - 1), unsel_t, 1.0)).astype(BF16)
        unsel_tail = jnp.transpose(unsel_t).astype(BF16)

        q2s.append(q2)
        o_cmps.append(o_cmp)
        lhs_mains.append(jnp.concatenate([q2, jnp.concatenate([unsel_main] * R_A, axis=0)], axis=1))
        lhs_tails.append(jnp.concatenate([q2, jnp.concatenate([unsel_tail] * R_A, axis=0)], axis=1))

    def score_tile(kt, ms):
        c0 = pl.multiple_of(kt * KT, KT)
        out = []
        for g in range(G_A):
            s = _dot(lhs_mains[g], k_ext(g, c0, KT))
            sc_ref[g, :, pl.ds(c0, KT)] = s
            out.append(jnp.maximum(ms[g], jnp.max(s, axis=-1, keepdims=True)))
        return tuple(out)

    ms = lax.fori_loop(0, n_main, score_tile, tuple(jnp.full((R_A * TQ, 1), NEG, F32) for _ in range(G_A)))
    tails = []
    for g in range(G_A):
        d_here = dsel_ref[R_A * g:R_A * (g + 1)].reshape(R_A * TQ, 2 * TQ)
        s_a = _dot(lhs_tails[g], k_ext(g, c_a, TQ)) + d_here[:, 0:TQ] + jnp.where(i == 0, NEG, 0.0)
        s_b = _dot(lhs_tails[g], k_ext(g, c_b, TQ)) + d_here[:, TQ:2 * TQ]
        m = jnp.maximum(ms[g], jnp.maximum(jnp.max(s_a, axis=-1, keepdims=True),
                                           jnp.max(s_b, axis=-1, keepdims=True)))
        tails.append((s_a, s_b, m))

    def pv_tile(kt, accs):
        c0 = pl.multiple_of(kt * KT, KT)
        out = []
        for g in range(G_A):
            p = jnp.exp(sc_ref[g, :, pl.ds(c0, KT)] - tails[g][2]).astype(BF16)
            out.append(accs[g] + _dot_nt(p, v_ext(g, c0, KT)))
        return tuple(out)

    accs = lax.fori_loop(0, n_main, pv_tile,
                         tuple(jnp.zeros((R_A * TQ, HD_A + 16), F32) for _ in range(G_A)))

    for g in range(G_A):
        q2, o_cmp = q2s[g], o_cmps[g]
        s_a, s_b, m = tails[g]
        acc = (accs[g] + _dot_nt(jnp.exp(s_a - m).astype(BF16), v_ext(g, c_a, TQ))
               + _dot_nt(jnp.exp(s_b - m).astype(BF16), v_ext(g, c_b, TQ)))

        s_w = []
        for w in range(n_win_tiles):
            kt = i - (n_win_tiles - 1) + w
            c0 = pl.multiple_of(jnp.maximum(kt, 0) * TQ, TQ)
            kw = win_ref[0, HD_A * g:HD_A * (g + 1), pl.ds(c0, TQ)]
            bw = bwin_ref[R_A * g:R_A * (g + 1), :, w * TQ:(w + 1) * TQ].reshape(R_A * TQ, TQ)
            s_w.append(_dot(q2, kw) + bw + jnp.where(kt < 0, NEG, 0.0))
        s_w = jnp.concatenate(s_w, axis=1)
        p_w = jnp.exp(s_w - jnp.max(s_w, axis=-1, keepdims=True)).astype(BF16)
        acc_w = jnp.zeros((R_A * TQ, HD_A + 16), F32)
        for w in range(n_win_tiles):
            kt = i - (n_win_tiles - 1) + w
            c0 = pl.multiple_of(jnp.maximum(kt, 0) * TQ, TQ)
            vw = jnp.concatenate([win_ref[0, LANE + HD_A * g:LANE + HD_A * (g + 1), pl.ds(c0, TQ)],
                                  ones_rows[:, 0:TQ]], axis=0)
            acc_w = acc_w + _dot_nt(p_w[:, w * TQ:(w + 1) * TQ], vw)

        for r in range(R_A):
            h = R_A * g + r
            rows = slice(r * TQ, (r + 1) * TQ)
            c_sel = gates[:, 3 * h + 1:3 * h + 2] / acc[rows, HD_A:HD_A + 1]
            c_win = gates[:, 3 * h + 2:3 * h + 3] / acc_w[rows, HD_A:HD_A + 1]
            o_ref[:, HD_A * h:HD_A * (h + 1)] = (gates[:, 3 * h:3 * h + 1] * o_cmp[rows]
                                                  + c_sel * acc[rows, 0:HD_A] + c_win * acc_w[rows, 0:HD_A])


def _prompt_nsa(rel_bias, qa, kvb, cmpk, gates, movt, emat):
    nb, _, t = kvb.shape
    ni = t // TQ
    nch = cmpk.shape[1]
    n_win_tiles = WINDOW // TQ + 1

    def full(a):
        return pl.BlockSpec(a.shape, lambda i, b, _n=a.ndim: (0,) * _n)

    return pl.pallas_call(
        _prompt_nsa_kernel,
        grid=(ni, nb),
        in_specs=[pl.BlockSpec(memory_space=pltpu.SMEM),
                  pl.BlockSpec((TQ, W_A), lambda i, b: (b * ni + i, 0)),
                  pl.BlockSpec((1, 2 * LANE, t), lambda i, b: (b, 1, 0)),
                  pl.BlockSpec((1, 2 * LANE, t), lambda i, b: (b, 2, 0)),
                  pl.BlockSpec((1, nch, 2 * LANE), lambda i, b: (b, 0, 0)),
                  pl.BlockSpec((TQ, LANE), lambda i, b: (b * ni + i, 0)),
                  full(movt), full(emat)],
        out_specs=pl.BlockSpec((TQ, W_A), lambda i, b: (b * ni + i, 0)),
        out_shape=jax.ShapeDtypeStruct((nb * t, W_A), F32),
        scratch_shapes=[pltpu.VMEM((H_A, TQ, n_win_tiles * TQ), F32),
                        pltpu.VMEM((H_A, TQ, 2 * TQ), F32),
                        pltpu.VMEM((H_A, TQ, 2 * nch), F32),
                        pltpu.VMEM((H_A, TQ, nch), F32),
                        pltpu.VMEM((G_A, R_A * TQ, t), F32)],
        compiler_params=pltpu.CompilerParams(vmem_limit_bytes=VMEM_LIMIT,
                                             dimension_semantics=("arbitrary", "arbitrary")),
        name="prompt_nsa",
    )(rel_bias, qa, kvb, kvb, cmpk, gates, movt, emat)


def _prompt_mla_kernel(qm_ref, mla_ref, o_ref, sc_ref, acc_ref):
    i = pl.program_id(1)
    nrow = H_B * TQM
    n_full = (i * TQM) // KT
    q2 = jnp.concatenate([qm_ref[:, 2 * LANE * h:2 * LANE * (h + 1)] for h in range(H_B)], axis=0)

    def score_tile(kt, m):
        c0 = pl.multiple_of(kt * KT, KT)
        s = _dot(q2, mla_ref[0, :, pl.ds(c0, KT)])
        sc_ref[:, pl.ds(c0, KT)] = s
        return jnp.maximum(m, jnp.max(s, axis=-1, keepdims=True))

    m = lax.fori_loop(0, n_full, score_tile, jnp.full((nrow, 1), NEG, F32))
    c_d = pl.multiple_of(n_full * KT, KT)
    qpos = i * TQM + lax.broadcasted_iota(jnp.int32, (H_B, TQM, KT), 1).reshape(nrow, KT)
    kpos = c_d + lax.broadcasted_iota(jnp.int32, (nrow, KT), 1)
    s_d = jnp.where(kpos <= qpos, _dot(q2, mla_ref[0, :, pl.ds(c_d, KT)]), NEG)
    m = jnp.maximum(m, jnp.max(s_d, axis=-1, keepdims=True))
    acc_ref[...] = _dot_nt(jnp.exp(s_d - m).astype(BF16), mla_ref[0, :, pl.ds(c_d, KT)])

    def pv_tile(kt, carry):
        c0 = pl.multiple_of(kt * KT, KT)
        p = jnp.exp(sc_ref[:, pl.ds(c0, KT)] - m).astype(BF16)
        acc_ref[...] += _dot_nt(p, mla_ref[0, :, pl.ds(c0, KT)])
        return carry

    lax.fori_loop(0, n_full, pv_tile, 0)
    acc = acc_ref[...]
    o = acc[:, 0:KV_LORA] / acc[:, KV_LORA + ROPE_DIM:KV_LORA + ROPE_DIM + 1]
    for h in range(H_B):
        o_ref[:, KV_LORA * h:KV_LORA * (h + 1)] = o[h * TQM:(h + 1) * TQM]


def _prompt_mla(qm, mlab):
    nb, rows, t = mlab.shape
    ni = t // TQM
    return pl.pallas_call(
        _prompt_mla_kernel,
        grid=(nb, ni),
        in_specs=[pl.BlockSpec((TQM, H_B * 2 * LANE), lambda b, i: (b * ni + i, 0)),
                  pl.BlockSpec((1, rows, t), lambda b, i: (b, 0, 0))],
        out_specs=pl.BlockSpec((TQM, H_B * KV_LORA), lambda b, i: (b * ni + i, 0)),
        out_shape=jax.ShapeDtypeStruct((nb * t, H_B * KV_LORA), F32),
        scratch_shapes=[pltpu.VMEM((H_B * TQM, t), F32), pltpu.VMEM((H_B * TQM, rows), F32)],
        compiler_params=pltpu.CompilerParams(vmem_limit_bytes=VMEM_LIMIT),
        name="prompt_mla",
    )(qm, mlab)


def _merge_kernel(x_ref, gate_ref, oa_ref, ol_ref, sza_ref, szb_ref, wuv_ref, wout_ref, gf_ref, y_ref):
    ya = oa_ref[...] * sza_ref[...]
    ol = ol_ref[...]
    szb = szb_ref[...]
    yb = [_dot(ol[:, KV_LORA * h:KV_LORA * (h + 1)].astype(BF16), wuv_ref[h]) * szb[:, V_HEAD * h:V_HEAD * (h + 1)]
          for h in range(H_B)]
    yab = jnp.concatenate([ya] + yb, axis=1).astype(BF16)
    xn = x_ref[...] + gate_ref[0] * _dot(yab, wout_ref[...])
    y_ref[...] = xn * lax.rsqrt(jnp.mean(xn * xn, axis=-1, keepdims=True) + EPS) * gf_ref[...]


def _merge(xf, gate3, per_row_mod, tb, oa, ol, sza, szb, wuv, wout, gfin):
    n, d = xf.shape
    tm = _row_tile(tb)
    tiles_b = tb // tm
    if per_row_mod:
        mod_spec = pl.BlockSpec((1, tm, d), lambda r: (r, 0, 0))
    else:
        mod_spec = pl.BlockSpec((1, 1, d), lambda r: (r // tiles_b, 0, 0))

    def full(a):
        return pl.BlockSpec(a.shape, lambda r, _n=a.ndim: (0,) * _n)

    def tmaj(cols):
        return pl.BlockSpec((tm, cols), lambda r: (r, 0))

    return pl.pallas_call(
        _merge_kernel,
        grid=(n // tm,),
        in_specs=[tmaj(d), mod_spec, tmaj(W_A), tmaj(W_B), tmaj(W_A), tmaj(W_B), full(wuv), full(wout), full(gfin)],
        out_specs=tmaj(d),
        out_shape=jax.ShapeDtypeStruct((n, d), F32),
        compiler_params=pltpu.CompilerParams(vmem_limit_bytes=VMEM_LIMIT),
        name="out_merge",
    )(xf, gate3, oa, ol, sza, szb, wuv, wout, gfin)


def _row_head_bias(bucket, tab_ref, g, shift):
    ts = bucket.shape[0] // R_A
    row_r = lax.broadcasted_iota(jnp.int32, bucket.shape, 0) // ts
    out = jnp.zeros(bucket.shape, F32)
    for r in range(R_A):
        h = R_A * g + r
        val = _bias_lookup(bucket, tab_ref, h)
        if shift:
            val = val - tab_ref[N_BUCKETS - 1, h]
        out = jnp.where(row_r == r, val, out)
    return out


def _page_copy(pt_ref, hbm, buf_ref, sem_ref, seq, grp, slot, k):
    return pltpu.make_async_copy(hbm.at[pt_ref[seq, grp * DEC_GROUP + k]], buf_ref.at[slot, k], sem_ref.at[slot])


def _own_lanes(x, b, ts):
    per_block = LANE // ts
    shift = (LANE - ts * (b % per_block)) % LANE
    lane = lax.broadcasted_iota(jnp.int32, x.shape, 1)
    return jnp.where(lane < ts, pltpu.roll(x, shift, 1), 0.0)


def _decode_kernel(pt_ref, tab_ref, nsa_hbm, mla_hbm, qa_ref, nsel_ref, nwin_ref, win_ref, gsel_ref, qm_ref,
                   newm_ref, lperm_ref, wc_ref, pe_ref, wflat_ref, mov_ref, o_ref, wout_ref, om_ref,
                   a_ref, stash_ref, bc_ref, ds_ref, dn_ref, bw_ref, bwn_ref, nbuf_ref, mbuf_ref, nsem_ref,
                   msem_ref, *, ts, past):
    b = pl.program_id(0)
    nseq = pl.num_programs(0)
    ng = past // LANE // DEC_GROUP
    nch = past // CMP_STRIDE
    nc = nch - 1
    nrow = R_A * ts
    n_cmp_rows = 2 * G_A * HD_A
    wb = win_ref.shape[2]
    q_lat, q_rope = qm_ref[0][:, 0:KV_LORA], qm_ref[0][:, KV_LORA:KV_LORA + ROPE_DIM]

    def mla_scores(kt):
        return _dot(q_lat, kt[0:KV_LORA]) + _dot(q_rope, kt[KV_LORA:KV_LORA + ROPE_DIM])

    def start(seq, grp, slot):
        for k in range(DEC_GROUP):
            _page_copy(pt_ref, nsa_hbm, nbuf_ref, nsem_ref, seq, grp, slot, k).start()
            _page_copy(pt_ref, mla_hbm, mbuf_ref, msem_ref, seq, grp, slot, k).start()

    @pl.when(b == 0)
    def _():
        for q0 in range(DEC_AHEAD):
            start(0, q0, q0)

    mla_keys, mla_s = [], []
    for j in range(ng):
        slot = (b * ng + j) % DEC_RING
        seq_off, grp = divmod(j + DEC_AHEAD, ng)
        nxt = (b * ng + j + DEC_AHEAD) % DEC_RING
        if seq_off == 0:
            start(b, grp, nxt)
        else:
            pl.when(b + seq_off < nseq)(functools.partial(start, b + seq_off, grp, nxt))
        for k in range(DEC_GROUP):
            _page_copy(pt_ref, nsa_hbm, nbuf_ref, nsem_ref, b, j, slot, k).wait()
            _page_copy(pt_ref, mla_hbm, mbuf_ref, msem_ref, b, j, slot, k).wait()
        for pp in range(DEC_GROUP // 2):
            pair = j * (DEC_GROUP // 2) + pp
            pa = nbuf_ref[slot, 2 * pp]
            pb = nbuf_ref[slot, 2 * pp + 1]
            both = jnp.concatenate([pa[0:n_cmp_rows], pb[0:n_cmp_rows]], axis=1).astype(BF16)
            tb = _chunk_rows(lperm_ref, both)
            for jj_ in range(CMP_STRIDE):
                a_ref[jj_, pair * 16:(pair + 1) * 16, :] = tb[16 * jj_:16 * (jj_ + 1)]
            stash_ref[:, pair * 256:(pair + 1) * 256] = jnp.concatenate(
                [pa[n_cmp_rows:2 * n_cmp_rows], pb[n_cmp_rows:2 * n_cmp_rows]], axis=1).astype(BF16)
        kt = jnp.concatenate([mbuf_ref[slot, k].astype(BF16) for k in range(DEC_GROUP)], axis=1)
        mla_keys.append(kt)
        mla_s.append(mla_scores(kt))

    trow = lax.broadcasted_iota(jnp.int32, (nrow, 1), 0) % ts

    @pl.when(b == 0)
    def _():
        n = lax.broadcasted_iota(jnp.int32, (nrow, nch), 1)
        dist = past + trow - (n * CMP_STRIDE + CMP_BLOCK - 1)
        u = lax.broadcasted_iota(jnp.int32, (nrow, LANE), 1)
        dist_s = LANE + trow - u
        dist_n = trow - u
        kl = lax.broadcasted_iota(jnp.int32, (nrow, wb), 1)
        dist_w = wb + trow - kl
        ok_n = (dist_n >= 0) & (u < ts)
        for g in range(G_A):
            bc_ref[g] = jnp.where((dist >= 0) & (n < nc), _row_head_bias(_rel_bucket(dist), tab_ref, g, False), NEG)
            ds_ref[g] = _row_head_bias(_rel_bucket(dist_s), tab_ref, g, True)
            dn_ref[g] = jnp.where(ok_n, _row_head_bias(_rel_bucket(dist_n), tab_ref, g, True), NEG)
            bw_ref[g] = jnp.where(dist_w < WINDOW, _row_head_bias(_rel_bucket(dist_w), tab_ref, g, False), NEG)
            bwn_ref[g] = jnp.where(ok_n, _row_head_bias(_rel_bucket(dist_n), tab_ref, g, False), NEG)

    kn = _own_lanes(newm_ref[0], b, ts).astype(BF16)
    hrow = lax.broadcasted_iota(jnp.int32, (H_B * ts, LANE), 0) % ts
    ulane = lax.broadcasted_iota(jnp.int32, (H_B * ts, LANE), 1)
    mla_s.append(mla_scores(kn) + jnp.where((ulane <= hrow) & (ulane < ts), 0.0, NEG))
    s_m = jnp.concatenate(mla_s, axis=1)
    p_m = jnp.exp(s_m - jnp.max(s_m, axis=-1, keepdims=True))
    l_m = jnp.sum(p_m, axis=-1, keepdims=True)
    p_mb = p_m.astype(BF16)
    gw = DEC_GROUP * LANE
    acc_m = _dot_nt(p_mb[:, ng * gw:ng * gw + LANE], kn[0:KV_LORA])
    for j, kt in enumerate(mla_keys):
        acc_m = acc_m + _dot_nt(p_mb[:, j * gw:(j + 1) * gw], kt[0:KV_LORA])
    om_ref[0] = acc_m / l_m

    cmpk = _compress_chunks(a_ref, wc_ref, pe_ref, wflat_ref, nch).astype(BF16)
    new_sel = _own_lanes(nsel_ref[0], b, ts)
    new_win = _own_lanes(nwin_ref[0], b, ts)
    nsp = mov_ref.shape[1]
    nbp = past // SEL_BLOCK
    jj = lax.broadcasted_iota(jnp.int32, (nrow, nsp), 1)
    cc = (past + trow) // SEL_BLOCK
    o_cmps, tots = [], []
    for g in range(G_A):
        q2 = qa_ref[0, g]
        s_c = _dot_nt(q2, cmpk[:, HD_A * g:HD_A * (g + 1)]) + bc_ref[g]
        p_c = _softmax_rows(s_c).astype(BF16)
        o_cmps.append(_dot(p_c, cmpk[:, LANE + HD_A * g:LANE + HD_A * (g + 1)]))
        imp = _dot(p_c, mov_ref[...])
        tot = imp
        for r in range(1, R_A):
            tot = tot + pltpu.roll(imp, r * ts, 0)
        tots.append(tot)
    half = nrow // 2
    first = lax.broadcasted_iota(jnp.int32, (half, nsp), 0) < ts
    jj8, cc8 = jj[0:half], cc[0:half]
    forced = (jj8 == 0) | (jj8 == cc8) | (jj8 == cc8 - 1)
    mixed = jnp.where(first, tots[0][0:half], tots[1][0:half])
    score = jnp.where(forced, FORCE_SCORE, jnp.where(jj8 <= cc8, mixed, -jnp.inf))
    rank = _rank(score, nbp + 1, 1)
    rank_sw = pltpu.roll(rank, ts, 0)
    allowed = jj8 <= cc8
    sel8 = [((jnp.where(first, rank, rank_sw) < N_SEL) & allowed).astype(F32)[:, 0:nbp],
            ((jnp.where(first, rank_sw, rank) < N_SEL) & allowed).astype(F32)[:, 0:nbp]]
    low = lax.broadcasted_iota(jnp.int32, (half, LANE), 1) < SEL_BLOCK
    madds = []
    for g in range(G_A):
        flags = [jnp.where(low, sel8[g][:, 2 * c:2 * c + 1], sel8[g][:, 2 * c + 1:2 * c + 2])
                 for c in range(nbp // 2)]
        m8 = (jnp.concatenate(flags, axis=1) - 1.0) * (-NEG)
        madds.append(jnp.concatenate([m8, m8], axis=0))
    for g in range(G_A):
        q2 = qa_ref[0, g]
        o_cmp = o_cmps[g]
        s_p = _dot(q2, stash_ref[HD_A * g:HD_A * (g + 1), :]) + madds[g]
        s_last = s_p[:, past - LANE:past] + ds_ref[g]
        k_new = new_sel[HD_A * g:HD_A * (g + 1)].astype(BF16)
        v_new = new_sel[LANE + HD_A * g:LANE + HD_A * (g + 1)].astype(BF16)
        s_n = _dot(q2, k_new) + dn_ref[g]
        s_all = jnp.concatenate([s_p[:, 0:past - LANE], s_last, s_n], axis=1)
        m = jnp.max(s_all, axis=-1, keepdims=True)
        p = jnp.exp(s_all - m)
        l = jnp.sum(p, axis=-1, keepdims=True)
        pb16 = p.astype(BF16)
        o_sel = (_dot_nt(pb16[:, 0:past], stash_ref[LANE + HD_A * g:LANE + HD_A * (g + 1), :])
                 + _dot_nt(pb16[:, past:past + LANE], v_new)) / l
        kw = win_ref[0, HD_A * g:HD_A * (g + 1), :].astype(BF16)
        vw = win_ref[0, LANE + HD_A * g:LANE + HD_A * (g + 1), :].astype(BF16)
        kw_new = new_win[HD_A * g:HD_A * (g + 1)].astype(BF16)
        vw_new = new_win[LANE + HD_A * g:LANE + HD_A * (g + 1)].astype(BF16)
        s_w = jnp.concatenate([_dot(q2, kw) + bw_ref[g], _dot(q2, kw_new) + bwn_ref[g]], axis=1)
        p_w = _softmax_rows(s_w).astype(BF16)
        o_win = _dot_nt(p_w[:, 0:wb], vw) + _dot_nt(p_w[:, wb:wb + LANE], vw_new)
        gs = gsel_ref[0, g]
        o_ref[0, g] = gs[:, 0:1] * o_cmp + gs[:, 1:2] * o_sel + gs[:, 2:3] * o_win

    win = win_ref[0]
    rolled = pltpu.roll(win, wb - ts, 1)
    new_w = pltpu.roll(new_win, LANE - ts, 1)
    lane = lax.broadcasted_iota(jnp.int32, (win.shape[0], LANE), 1)
    last = jnp.where(lane >= LANE - ts, new_w, rolled[:, wb - LANE:wb])
    wout_ref[0] = jnp.concatenate([rolled[:, 0:wb - LANE], last], axis=1)


def _decode_attention(page_table, rel_bias, cache_nsa_t, cache_mla_t, qa_s, nsat_s, wint_s, win_t, gsel, qm_s,
                      mlat_s, cw, mov, ts, past):
    bd, n_pages = page_table.shape
    lperm, wc, pe, wflat = cw
    nrow = R_A * ts
    mrow = H_B * ts
    nch = past // CMP_STRIDE
    wb = win_t.shape[2]
    feat_n, feat_m = cache_nsa_t.shape[1], cache_mla_t.shape[1]
    per_block = LANE // ts

    def full(a):
        return pl.BlockSpec(a.shape, lambda b, pt, _n=a.ndim: (0,) * _n, pipeline_mode=pl.Buffered(1))

    in_specs = [pl.BlockSpec(memory_space=pltpu.SMEM),
                pl.BlockSpec(memory_space=pl.ANY),
                pl.BlockSpec(memory_space=pl.ANY),
                pl.BlockSpec((1, G_A, nrow, HD_A), lambda b, pt: (b, 0, 0, 0)),
                pl.BlockSpec((1, 2 * LANE, LANE), lambda b, pt: (0, 1, b // per_block)),
                pl.BlockSpec((1, 2 * LANE, LANE), lambda b, pt: (0, 0, b // per_block)),
                pl.BlockSpec((1, 2 * LANE, wb), lambda b, pt: (b, 0, 0)),
                pl.BlockSpec((1, G_A, nrow, LANE), lambda b, pt: (b, 0, 0, 0)),
                pl.BlockSpec((1, mrow, 2 * LANE), lambda b, pt: (b, 0, 0)),
                pl.BlockSpec((1, feat_m, LANE), lambda b, pt: (0, 0, b // per_block)),
                full(lperm), full(wc), full(pe), full(wflat), full(mov)]
    grid_spec = pltpu.PrefetchScalarGridSpec(
        num_scalar_prefetch=1,
        grid=(bd,),
        in_specs=in_specs,
        out_specs=[pl.BlockSpec((1, G_A, nrow, HD_A), lambda b, pt: (b, 0, 0, 0)),
                   pl.BlockSpec((1, 2 * LANE, wb), lambda b, pt: (b, 0, 0)),
                   pl.BlockSpec((1, mrow, KV_LORA), lambda b, pt: (b, 0, 0))],
        scratch_shapes=[pltpu.VMEM((CMP_STRIDE, nch, 2 * LANE), BF16),
                        pltpu.VMEM((2 * LANE, past), BF16),
                        pltpu.VMEM((G_A, nrow, nch), F32),
                        pltpu.VMEM((G_A, nrow, LANE), F32),
                        pltpu.VMEM((G_A, nrow, LANE), F32),
                        pltpu.VMEM((G_A, nrow, wb), F32),
                        pltpu.VMEM((G_A, nrow, LANE), F32),
                        pltpu.VMEM((DEC_RING, DEC_GROUP, feat_n, LANE), F32),
                        pltpu.VMEM((DEC_RING, DEC_GROUP, feat_m, LANE), F32),
                        pltpu.SemaphoreType.DMA((DEC_RING,)),
                        pltpu.SemaphoreType.DMA((DEC_RING,))])
    return pl.pallas_call(
        functools.partial(_decode_kernel, ts=ts, past=past),
        grid_spec=grid_spec,
        out_shape=(jax.ShapeDtypeStruct((bd, G_A, nrow, HD_A), F32),
                   jax.ShapeDtypeStruct((bd, 2 * LANE, wb), F32),
                   jax.ShapeDtypeStruct((bd, mrow, KV_LORA), F32)),
        compiler_params=pltpu.CompilerParams(vmem_limit_bytes=VMEM_LIMIT,
                                             dimension_semantics=("arbitrary",)),
        name="decode_attn",
    )(page_table, rel_bias, cache_nsa_t, cache_mla_t, qa_s, nsat_s, wint_s, win_t, gsel, qm_s, mlat_s,
      lperm, wc, pe, wflat, mov)


def _rope_tables(pos):
    half = ROPE_DIM // 2
    inv = ROPE_BASE ** (-jnp.arange(half, dtype=F32) / half)
    ang = pos.astype(F32)[:, None] * inv[None, :]
    cos, sin = jnp.cos(ang), jnp.sin(ang)
    z = jnp.zeros((pos.shape[0], LANE - ROPE_DIM), F32)
    cos_tm = jnp.concatenate([cos, cos, z], axis=1)
    sin_tm = jnp.concatenate([sin, sin, z], axis=1)
    return cos_tm, sin_tm, cos.T, sin.T


def _overlap_t(n_cmp_rows, n_sel):
    cs = jnp.arange(n_cmp_rows)[None, :] * CMP_STRIDE
    ss = jnp.arange(n_sel)[:, None] * SEL_BLOCK
    ov = jnp.minimum(cs + CMP_BLOCK, ss + SEL_BLOCK) - jnp.maximum(cs, ss)
    ov = jnp.maximum(ov, 0).astype(F32) / CMP_BLOCK
    return jnp.where(jnp.arange(n_cmp_rows)[None, :] < n_cmp_rows - 1, ov, 0.0)


def _compress_weights(w_k, w_v, pe_k, pe_v):
    eye = jnp.eye(G_A, dtype=F32)

    def pair_blocks(w):
        w6 = w.reshape(2, CMP_STRIDE // 2, 2, HD_A, HD_A)
        blk = jnp.einsum('hpjde,gk->pjgdhke', w6, eye)
        return blk.reshape(CMP_STRIDE // 2, 2 * LANE, 2 * LANE)

    wc = jnp.stack([pair_blocks(w_k), pair_blocks(w_v)]).astype(BF16)
    pe = jnp.stack([jnp.tile(pe_k.reshape(1, -1), (8, 1)), jnp.tile(pe_v.reshape(1, -1), (8, 1))]).astype(BF16)
    wflat = jnp.stack([jnp.tile(w_k.reshape(-1, HD_A), (1, G_A)),
                       jnp.tile(w_v.reshape(-1, HD_A), (1, G_A))]).astype(BF16)
    r = jnp.arange(256)
    lperm = (jnp.arange(256)[None, :] == ((r % 16) * 16 + r // 16)[:, None]).astype(BF16)
    return lperm, wc, pe, wflat


def kernel(x_prompt, x_sample, cache_nsa, cache_mla, state_win, page_table, c_prompt, c_sample,
           w_ada, b_ada, g_norm, w_in, w_cmp_k, w_cmp_v, pe_cmp_k, pe_cmp_v, rel_bias,
           g_q_lat, w_uq, g_kv_lat, w_uk, w_uv, w_out, g_final):
    assert w_in.shape[0] == 1, "one layer"
    nb, t, d = x_prompt.shape
    bd, ts, _ = x_sample.shape
    n_pages = page_table.shape[1]
    page = cache_nsa.shape[2]
    past = n_pages * page
    assert page == LANE and (past + ts) // CMP_STRIDE == past // CMP_STRIDE and ts <= SEL_BLOCK
    assert (R_A * ts // 2) % ts == 0 and t % TQM == 0 and LANE % ts == 0 and (bd * ts) % LANE == 0
    assert t % KT == 0 and (bd * ts) % (TM // 2) == 0 and n_pages % DEC_GROUP == 0 and n_pages // DEC_GROUP >= DEC_AHEAD
    wb = state_win.shape[2]

    wi = w_in[0]
    o1 = W_A
    o2 = o1 + 6 * G_A * HD_A
    o3 = o2 + 3 * H_A
    o4 = o3 + W_A
    o5 = o4 + Q_LORA
    o6 = o5 + KV_LORA
    o7 = o6 + ROPE_DIM
    wg = jnp.pad(wi[:, o2:o3], ((0, 0), (0, LANE - 3 * H_A)))
    wt = jnp.concatenate([wi[:, 0:o1] * (HD_A ** -0.5), wi[:, o3:o4], wi[:, o7:], wi[:, o4:o5], wg], axis=1).astype(BF16)
    wf = jnp.concatenate([wi[:, o1:o2], wi[:, o5:o6], wi[:, o6:o7]], axis=1).T.astype(BF16)
    wq = w_uq[0]
    wqn = wq[:, :, :NOPE].reshape(Q_LORA, H_B * NOPE).astype(BF16)
    wr = wq[:, :, NOPE:]
    half = ROPE_DIM // 2
    wsw = jnp.concatenate([-wr[..., half:], wr[..., :half]], axis=-1)
    padr = ((0, 0), (0, 0), (0, LANE - ROPE_DIM))
    wqr = jnp.pad(wr, padr).reshape(Q_LORA, H_B * LANE).astype(BF16)
    wqs = jnp.pad(wsw, padr).reshape(Q_LORA, H_B * LANE).astype(BF16)
    wuk = jnp.transpose(w_uk[0], (1, 2, 0)).astype(BF16)
    wuv = jnp.transpose(w_uv[0], (1, 0, 2)).astype(BF16)
    wts = (g_norm[0][None], wt, wf, g_q_lat[0][None], wqn, wqr, wqs, wuk, g_kv_lat[0][:, None])
    cw = _compress_weights(w_cmp_k[0], w_cmp_v[0], pe_cmp_k[0], pe_cmp_v[0])
    wout = w_out[0].astype(BF16)
    gfin = g_final[None]

    n_c = nb + bd
    n_cp = -(-n_c // 8) * 8
    c_all = jnp.pad(jnp.concatenate([c_prompt, c_sample], axis=0), ((0, n_cp - n_c), (0, 0)))
    mod = _modulation(c_all, w_ada[0], b_ada)
    shift, scale, gate = mod[:, 0:d], mod[:, d:2 * d], mod[:, 2 * d:3 * d]

    def prompt_mod(a):
        return a[0:nb][:, None, :]

    def sample_mod(a):
        tm_s = _row_tile(bd * ts)
        return jnp.repeat(a[nb:nb + bd], ts, axis=0).reshape(bd * ts // tm_s, tm_s, d)

    pos_p = jnp.arange(t, dtype=jnp.int32)
    xp = x_prompt.reshape(nb * t, d)
    (qa, gates, sza, szb, qm, nsat, wint, kvb, mlat, mlab) = _project(
        xp, prompt_mod(shift), prompt_mod(scale), False, nb, t, wts, *_rope_tables(pos_p))
    cmpk = _prompt_compress(kvb, cw)
    nch_p = t // CMP_STRIDE
    ns_p = t // SEL_BLOCK
    movt = _overlap_t(nch_p, ns_p).astype(BF16)
    emat = jnp.where(jnp.arange(t)[None, :] // SEL_BLOCK == jnp.arange(ns_p)[:, None], NEG, 0.0).astype(BF16)
    oa = _prompt_nsa(rel_bias, qa, kvb, cmpk, gates, movt, emat)
    ol = _prompt_mla(qm, mlab)
    y_prompt = _merge(xp, prompt_mod(gate), False, t, oa, ol, sza, szb, wuv, wout, gfin).reshape(nb, t, d)

    n_s = bd * ts
    pos_s = past + jnp.arange(ts, dtype=jnp.int32)
    cos_tm, sin_tm, cos_t, sin_t = _rope_tables(pos_s)
    xs = x_sample.reshape(n_s, d)
    (qa_s, gates_s, sza_s, szb_s, qm_s, nsat_s, wint_s, _, mlat_s, _) = _project(
        xs, sample_mod(shift), sample_mod(scale), True, 1, n_s, wts,
        jnp.tile(cos_tm, (bd, 1)), jnp.tile(sin_tm, (bd, 1)), jnp.tile(cos_t, (1, bd)), jnp.tile(sin_t, (1, bd)))
    nrow = R_A * ts
    qa_g = qa_s.reshape(bd, ts, G_A, R_A, HD_A).transpose(0, 2, 3, 1, 4).reshape(bd, G_A, nrow, HD_A)
    gsel = gates_s[:, 0:3 * H_A].reshape(bd, ts, G_A, R_A, 3).transpose(0, 2, 3, 1, 4).reshape(bd, G_A, nrow, 3)
    gsel = jnp.pad(gsel, ((0, 0), (0, 0), (0, 0), (0, LANE - 3)))
    qm_h = qm_s.reshape(bd, ts, H_B, 2 * LANE).transpose(0, 2, 1, 3).reshape(bd, H_B * ts, 2 * LANE)
    n_phys = cache_nsa.shape[1]
    cache_nsa_t = jnp.transpose(cache_nsa[0], (0, 2, 3, 4, 1)).reshape(n_phys, 4 * G_A * HD_A, page)
    cache_mla_t = jnp.transpose(cache_mla[0], (0, 2, 1))
    win_t = jnp.transpose(state_win[0], (0, 2, 3, 4, 1)).reshape(bd, 2 * G_A * HD_A, wb)
    nch_s = past // CMP_STRIDE
    nbp = past // SEL_BLOCK
    nsp = -(-(nbp + 1) // LANE) * LANE
    mov_s = jnp.pad(_overlap_t(nch_s, nbp + 1).T, ((0, 0), (0, nsp - nbp - 1))).astype(BF16)
    oa_g, win_new, ol_h = _decode_attention(page_table, rel_bias, cache_nsa_t, cache_mla_t, qa_g, nsat_s, wint_s,
                                            win_t, gsel, qm_h, mlat_s, cw, mov_s, ts, past)
    oa_s = oa_g.reshape(bd, G_A, R_A, ts, HD_A).transpose(0, 3, 1, 2, 4).reshape(n_s, W_A)
    ol_s = ol_h.reshape(bd, H_B, ts, KV_LORA).transpose(0, 2, 1, 3).reshape(n_s, H_B * KV_LORA)
    y_sample = _merge(xs, sample_mod(gate), True, n_s, oa_s, ol_s, sza_s, szb_s, wuv, wout, gfin).reshape(bd, ts, d)

    new_nsa_prompt = nsat.reshape(nb, 4, G_A, HD_A, t).transpose(0, 4, 1, 2, 3)[None]
    new_nsa_sample = nsat_s[0].reshape(4, G_A, HD_A, bd, ts).transpose(3, 4, 0, 1, 2)[None]
    new_mla_prompt = mlat.transpose(0, 2, 1)[None]
    new_mla_sample = mlat_s[0].reshape(KV_LORA + ROPE_DIM, bd, ts).transpose(1, 2, 0)[None]
    wlen = min(WINDOW, t)
    new_win_prompt = wint[:, :, t - wlen:].reshape(nb, 2, G_A, HD_A, wlen).transpose(0, 4, 1, 2, 3)[None]
    new_win_sample = win_new.reshape(bd, 2, G_A, HD_A, wb).transpose(0, 4, 1, 2, 3)[None]
    return (y_prompt, y_sample, new_nsa_prompt, new_nsa_sample, new_mla_prompt, new_mla_sample,
            new_win_prompt, new_win_sample)
```

```python
import functools
import math

import jax
import jax.numpy as jnp
from jax import lax
from jax.experimental import pallas as pl
from jax.experimental.pallas import tpu as pltpu

F32 = jnp.float32
BF16 = jnp.bfloat16

H_A, G_A, R_A, HD_A = 8, 2, 4, 64
W_A = H_A * HD_A
CMP_BLOCK, CMP_STRIDE = 32, 16
SEL_BLOCK, N_SEL = 64, 16
WINDOW = 512
FORCE_SCORE = 1e4
H_B, NOPE, ROPE_DIM, V_HEAD, Q_LORA, KV_LORA = 4, 128, 64, 128, 256, 128
W_B = H_B * V_HEAD
ROPE_BASE = 10000.0
N_BUCKETS, MAX_DIST = 32, 128
EPS = 1e-6
NEG = -1e30

TQ = 128
TQM = 256
KT = 1024
TM = 512
LANE = 128
MLA_PAGE_GROUP = 16
MLA_SLOTS = 3
VMEM_LIMIT = 56 * 1024 * 1024


def _row_tile(rows):
    return TM if rows % TM == 0 else TM // 2


def _dot(a, b):
    return jnp.dot(a, b, preferred_element_type=F32)


def _dot_nt(a, b):
    return lax.dot_general(a, b, (((1,), (1,)), ((), ())), preferred_element_type=F32)


def _rel_bucket(dist):
    n = jnp.maximum(dist, 0)
    max_exact = N_BUCKETS // 2
    span = N_BUCKETS - max_exact
    large = jnp.full(n.shape, max_exact, jnp.int32)
    for k in range(1, span):
        thr = math.ceil(max_exact * (MAX_DIST / max_exact) ** (k / span))
        large = large + (n >= thr).astype(jnp.int32)
    return jnp.where(n < max_exact, n, large)


def _bias_lookup(bucket, tab_ref, h):
    acc = jnp.zeros(bucket.shape, F32)
    for k in range(N_BUCKETS):
        acc = jnp.where(bucket == k, tab_ref[k, h], acc)
    return acc


def _softmax_rows(s):
    m = jnp.max(s, axis=-1, keepdims=True)
    p = jnp.exp(s - m)
    l = jnp.sum(p, axis=-1, keepdims=True)
    return jnp.where(s > 0.5 * NEG, p / l, 0.0)


def _mod_kernel(c_ref, w_ref, b_ref, o_ref):
    o_ref[...] = _dot(c_ref[...].astype(BF16), w_ref[...].astype(BF16)) + b_ref[...]


def _modulation(c, w_ada, b_ada):
    n, d = c.shape
    n3 = w_ada.shape[1]
    tn = 512
    return pl.pallas_call(
        _mod_kernel,
        grid=(n3 // tn,),
        in_specs=[pl.BlockSpec((n, d), lambda j: (0, 0)),
                  pl.BlockSpec((d, tn), lambda j: (0, j)),
                  pl.BlockSpec((1, tn), lambda j: (0, j))],
        out_specs=pl.BlockSpec((n, tn), lambda j: (0, j)),
        out_shape=jax.ShapeDtypeStruct((n, n3), F32),
        name="adaln_mod",
    )(c, w_ada, b_ada)


def _proj_kernel(x_ref, shift_ref, scale_ref, gn_ref, wt_ref, wf_ref, gq_ref, wqn_ref, wqr_ref,
                 wqs_ref, wuk_ref, gkv_ref, cos_ref, sin_ref, cost_ref, sint_ref,
                 qa_ref, gates_ref, sza_ref, szb_ref, qm_ref, nsat_ref, wint_ref, kvb_ref,
                 mlat_ref, mlab_ref, *, mla_scale):
    x = x_ref[...]
    y = x * lax.rsqrt(jnp.mean(x * x, axis=-1, keepdims=True) + EPS) * gn_ref[...]
    h = y * (1.0 + scale_ref[0]) + shift_ref[0]
    hb = h.astype(BF16)

    ut = _dot(hb, wt_ref[...])
    qa_ref[...] = ut[:, 0:W_A].astype(BF16)
    za = ut[:, W_A:2 * W_A]
    zb = ut[:, 2 * W_A:3 * W_A]
    sza_ref[...] = za * jax.nn.sigmoid(za)
    szb_ref[...] = zb * jax.nn.sigmoid(zb)
    cq = ut[:, 3 * W_A:3 * W_A + Q_LORA]
    gates_ref[...] = jax.nn.sigmoid(ut[:, 3 * W_A + Q_LORA:3 * W_A + Q_LORA + LANE])

    cqn = (cq * lax.rsqrt(jnp.mean(cq * cq, axis=-1, keepdims=True) + EPS) * gq_ref[...]).astype(BF16)
    qn = _dot(cqn, wqn_ref[...])
    qr = _dot(cqn, wqr_ref[...])
    qs = _dot(cqn, wqs_ref[...])
    cos_q = jnp.concatenate([cos_ref[...]] * H_B, axis=1)
    sin_q = jnp.concatenate([sin_ref[...]] * H_B, axis=1)
    q_rope = qr * cos_q + qs * sin_q
    for hh in range(H_B):
        q_lat = _dot(qn[:, hh * NOPE:(hh + 1) * NOPE].astype(BF16), wuk_ref[hh])
        qm_ref[:, 2 * LANE * hh:2 * LANE * hh + LANE] = (q_lat * mla_scale).astype(BF16)
        qm_ref[:, 2 * LANE * hh + LANE:2 * LANE * (hh + 1)] = (
            q_rope[:, hh * LANE:(hh + 1) * LANE] * mla_scale).astype(BF16)

    uf = _dot_nt(wf_ref[...], hb)
    n_sel_rows = 4 * G_A * HD_A
    n_kv_rows = 6 * G_A * HD_A
    nsat_ref[0] = uf[0:n_sel_rows]
    wint_ref[0] = uf[n_sel_rows:n_kv_rows]
    kvb_ref[0] = uf[0:n_kv_rows].astype(BF16)
    c = uf[n_kv_rows:n_kv_rows + KV_LORA]
    cn = c * lax.rsqrt(jnp.mean(c * c, axis=0, keepdims=True) + EPS) * gkv_ref[...]
    kr = uf[n_kv_rows + KV_LORA:n_kv_rows + KV_LORA + ROPE_DIM]
    half = ROPE_DIM // 2
    x1, x2 = kr[0:half], kr[half:ROPE_DIM]
    ct, st = cost_ref[...], sint_ref[...]
    mla = jnp.concatenate([cn, x1 * ct - x2 * st, x1 * st + x2 * ct], axis=0)
    mlat_ref[0] = mla
    n_pad = 2 * LANE - KV_LORA - ROPE_DIM
    pad = jnp.where(lax.broadcasted_iota(jnp.int32, (n_pad, mla.shape[1]), 0) == 0, 1.0, 0.0)
    mlab_ref[0] = jnp.concatenate([mla, pad], axis=0).astype(BF16)


def _project(xf, shift3, scale3, per_row_mod, nb, tb, wts, cos_tm, sin_tm, cos_t, sin_t):
    n, d = xf.shape
    tm = _row_tile(tb)
    tiles_b = tb // tm
    ntab = cos_tm.shape[0] // tm
    (gn, wt, wf, gq, wqn, wqr, wqs, wuk, gkv) = wts
    if per_row_mod:
        mod_spec = pl.BlockSpec((1, tm, d), lambda r: (r, 0, 0))
    else:
        mod_spec = pl.BlockSpec((1, 1, d), lambda r: (r // tiles_b, 0, 0))

    def full(a):
        return pl.BlockSpec(a.shape, lambda r, _n=a.ndim: (0,) * _n)

    def fm(rows):
        return pl.BlockSpec((1, rows, tm), lambda r: (r // tiles_b, 0, r % tiles_b))

    def tmaj(cols):
        return pl.BlockSpec((tm, cols), lambda r: (r, 0))

    n_sel_rows = 4 * G_A * HD_A
    out_shape = (
        jax.ShapeDtypeStruct((n, W_A), BF16),
        jax.ShapeDtypeStruct((n, LANE), F32),
        jax.ShapeDtypeStruct((n, W_A), F32),
        jax.ShapeDtypeStruct((n, W_B), F32),
        jax.ShapeDtypeStruct((n, H_B * 2 * LANE), BF16),
        jax.ShapeDtypeStruct((nb, n_sel_rows, tb), F32),
        jax.ShapeDtypeStruct((nb, 2 * G_A * HD_A, tb), F32),
        jax.ShapeDtypeStruct((nb, 6 * G_A * HD_A, tb), BF16),
        jax.ShapeDtypeStruct((nb, KV_LORA + ROPE_DIM, tb), F32),
        jax.ShapeDtypeStruct((nb, 2 * LANE, tb), BF16),
    )
    out_specs = (tmaj(W_A), tmaj(LANE), tmaj(W_A), tmaj(W_B), tmaj(H_B * 2 * LANE),
                 fm(n_sel_rows), fm(2 * G_A * HD_A), fm(6 * G_A * HD_A), fm(KV_LORA + ROPE_DIM), fm(2 * LANE))
    in_specs = [tmaj(d), mod_spec, mod_spec, full(gn), full(wt), full(wf), full(gq), full(wqn),
                full(wqr), full(wqs), full(wuk), full(gkv),
                pl.BlockSpec((tm, LANE), lambda r: (r % ntab, 0)),
                pl.BlockSpec((tm, LANE), lambda r: (r % ntab, 0)),
                pl.BlockSpec((ROPE_DIM // 2, tm), lambda r: (0, r % ntab)),
                pl.BlockSpec((ROPE_DIM // 2, tm), lambda r: (0, r % ntab))]
    return pl.pallas_call(
        functools.partial(_proj_kernel, mla_scale=(NOPE + ROPE_DIM) ** -0.5),
        grid=(n // tm,),
        in_specs=in_specs,
        out_specs=out_specs,
        out_shape=out_shape,
        compiler_params=pltpu.CompilerParams(vmem_limit_bytes=VMEM_LIMIT),
        name="in_proj",
    )(xf, shift3, scale3, gn, wt, wf, gq, wqn, wqr, wqs, wuk, gkv, cos_tm, sin_tm, cos_t, sin_t)


def _chunk_rows(lperm_ref, feats_tok):
    return _dot_nt(lperm_ref[...], feats_tok).astype(BF16)


def _compress_chunks(a_ref, wc_ref, pe_ref, wflat_ref, nch):
    outs = []
    for kind in range(2):
        acc = jnp.zeros((nch, 2 * LANE), F32)
        for p in range(CMP_STRIDE // 2):
            lhs = jnp.concatenate([a_ref[2 * p, :, kind * LANE:(kind + 1) * LANE],
                                   a_ref[2 * p + 1, :, kind * LANE:(kind + 1) * LANE]], axis=1)
            acc = acc + _dot(lhs, wc_ref[kind, p])
        first, second = acc[:, 0:LANE], acc[:, LANE:2 * LANE]
        cb = _dot(pe_ref[kind], wflat_ref[kind])[0:1]
        outs.append(first + pltpu.roll(second, nch - 1, 0) + cb)
    return jnp.concatenate(outs, axis=1)


def _prompt_compress_kernel(kv_ref, lperm_ref, wc_ref, pe_ref, wflat_ref, o_ref, a_ref):
    t = kv_ref.shape[2]
    nch = t // CMP_STRIDE
    for tix in range(t // 256):
        tb = _chunk_rows(lperm_ref, kv_ref[0, :, tix * 256:(tix + 1) * 256])
        for j in range(CMP_STRIDE):
            a_ref[j, tix * 16:(tix + 1) * 16, :] = tb[16 * j:16 * (j + 1)]
    o_ref[0] = _compress_chunks(a_ref, wc_ref, pe_ref, wflat_ref, nch).astype(BF16)


def _prompt_compress(kvb, cw):
    nb, _, t = kvb.shape
    nch = t // CMP_STRIDE
    lperm, wc, pe, wflat = cw

    def full(a):
        return pl.BlockSpec(a.shape, lambda b, _n=a.ndim: (0,) * _n)

    return pl.pallas_call(
        _prompt_compress_kernel,
        grid=(nb,),
        in_specs=[pl.BlockSpec((1, 2 * LANE, t), lambda b: (b, 0, 0)), full(lperm), full(wc), full(pe), full(wflat)],
        out_specs=pl.BlockSpec((1, nch, 2 * LANE), lambda b: (b, 0, 0)),
        out_shape=jax.ShapeDtypeStruct((nb, nch, 2 * LANE), BF16),
        scratch_shapes=[pltpu.VMEM((CMP_STRIDE, nch, 2 * LANE), BF16)],
        compiler_params=pltpu.CompilerParams(vmem_limit_bytes=VMEM_LIMIT),
        name="prompt_compress",
    )(kvb, lperm, wc, pe, wflat)


def _rank(score, n_valid, axis):
    size = 8 if axis == 0 else LANE
    ntile = score.shape[axis] // size

    def tile(a, k):
        return a[k * size:(k + 1) * size, :] if axis == 0 else a[:, k * size:(k + 1) * size]

    tiles = [tile(score, k) for k in range(ntile)]
    ranks = [jnp.zeros(tiles[0].shape, F32) for _ in range(ntile)]
    pos = lax.broadcasted_iota(jnp.int32, tiles[0].shape, axis)
    for j2 in range(n_valid):
        kt, off = divmod(j2, size)
        cand = tiles[kt][off:off + 1, :] if axis == 0 else tiles[kt][:, off:off + 1]
        for k in range(ntile):
            if k > kt:
                before = cand >= tiles[k]
            elif k < kt:
                before = cand > tiles[k]
            else:
                before = (cand > tiles[k]) | ((cand == tiles[k]) & (pos > off))
            ranks[k] = ranks[k] + jnp.where(before, 1.0, 0.0)
    return jnp.concatenate(ranks, axis=axis)


def _prompt_nsa_kernel(tab_ref, qa_ref, sel_ref, win_ref, cmp_ref, gates_ref, movt_ref, eneg_ref,
                       o_ref, bwin_ref, dsel_ref, gcmp_ref, dcmp_ref, sc_ref):
    i = pl.program_id(0)
    b = pl.program_id(1)
    t = sel_ref.shape[2]
    nch = cmp_ref.shape[1]
    ns = movt_ref.shape[0]
    n_win_tiles = WINDOW // TQ + 1
    chunks_per_tile = TQ // CMP_STRIDE
    m_off = chunks_per_tile * (t // TQ - 1)

    @pl.when((i == 0) & (b == 0))
    def _():
        tl = lax.broadcasted_iota(jnp.int32, (TQ, n_win_tiles * TQ), 0)
        kl = lax.broadcasted_iota(jnp.int32, (TQ, n_win_tiles * TQ), 1)
        dist = tl + WINDOW - kl
        bucket = _rel_bucket(dist)
        ok = (dist >= 0) & (dist < WINDOW)
        tl2 = lax.broadcasted_iota(jnp.int32, (TQ, 2 * TQ), 0)
        kl2 = lax.broadcasted_iota(jnp.int32, (TQ, 2 * TQ), 1)
        dist2 = tl2 + TQ - kl2
        bucket2 = _rel_bucket(dist2)
        for h in range(H_A):
            bwin_ref[h] = jnp.where(ok, _bias_lookup(bucket, tab_ref, h), NEG)
            dsel_ref[h] = jnp.where(dist2 >= 0,
                                    _bias_lookup(bucket2, tab_ref, h) - tab_ref[N_BUCKETS - 1, h], NEG)
        tl3 = lax.broadcasted_iota(jnp.int32, (TQ, 2 * nch), 0)
        m3 = lax.broadcasted_iota(jnp.int32, (TQ, 2 * nch), 1)
        dist3 = tl3 - (m3 - m_off) * CMP_STRIDE - (CMP_BLOCK - 1)
        bucket3 = _rel_bucket(dist3)
        for h in range(H_A):
            gcmp_ref[h] = jnp.where(dist3 >= 0,
                                    _bias_lookup(bucket3, tab_ref, h) - tab_ref[N_BUCKETS - 1, h], NEG)

    @pl.when(b == 0)
    def _():
        shift = (2 * nch - (m_off - chunks_per_tile * i)) % (2 * nch)
        for h in range(H_A):
            dcmp_ref[h] = pltpu.roll(gcmp_ref[h], shift, 1)[:, 0:nch]

    i_prev = jnp.maximum(i - 1, 0)
    n_main = (i_prev * TQ + KT - 1) // KT
    c_a = pl.multiple_of(i_prev * TQ, TQ)
    c_b = pl.multiple_of(i * TQ, TQ)
    blocks_per_tile = TQ // SEL_BLOCK
    ones_rows = jnp.ones((16, KT), BF16)
    gates = gates_ref[...]

    def k_ext(g, c0, width):
        return jnp.concatenate([sel_ref[0, HD_A * g:HD_A * (g + 1), pl.ds(c0, width)],
                                eneg_ref[:, pl.ds(c0, width)]], axis=0)

    def v_ext(g, c0, width):
        return jnp.concatenate([sel_ref[0, LANE + HD_A * g:LANE + HD_A * (g + 1), pl.ds(c0, width)],
                                ones_rows[:, 0:width]], axis=0)

    q2s, o_cmps, lhs_mains, lhs_tails = [], [], [], []
    for g in range(G_A):
        q2 = jnp.concatenate([qa_ref[:, HD_A * (R_A * g + r):HD_A * (R_A * g + r + 1)] for r in range(R_A)],
                             axis=0)
        kc = cmp_ref[0, :, HD_A * g:HD_A * (g + 1)]
        vc = cmp_ref[0, :, LANE + HD_A * g:LANE + HD_A * (g + 1)]
        s = _dot_nt(q2, kc).reshape(R_A, TQ, nch) + dcmp_ref[R_A * g:R_A * (g + 1)]
        p_c = _softmax_rows(s.reshape(R_A * TQ, nch))
        p_cb = p_c.astype(BF16)
        o_cmp = _dot(p_cb, vc)
        imp_t = jnp.zeros((ns, TQ), F32)
        for r in range(R_A):
            imp_t = imp_t + _dot_nt(movt_ref[...], p_cb[r * TQ:(r + 1) * TQ])
        jj = lax.broadcasted_iota(jnp.int32, (ns, TQ), 0)
        cc = (i * TQ + lax.broadcasted_iota(jnp.int32, (ns, TQ), 1)) // SEL_BLOCK
        forced = (jj == 0) | (jj == cc) | (jj == cc - 1)
        score = jnp.where(forced, FORCE_SCORE, jnp.where(jj <= cc, imp_t, -jnp.inf))
        rank = _rank(score, ns, 0)
        unsel_t = jnp.where((rank < min(N_SEL, ns)) & (jj <= cc), 0.0, 1.0)
        unsel_main = jnp.transpose(jnp.where(jj < blocks_per_tile * (i - 1), unsel_t, 1.0)).astype(BF16)
        unsel_tail = jnp.transpose(unsel_t).astype(BF16)

        q2s.append(q2)
        o_cmps.append(o_cmp)
        lhs_mains.append(jnp.concatenate([q2, jnp.concatenate([unsel_main] * R_A, axis=0)], axis=1))
        lhs_tails.append(jnp.concatenate([q2, jnp.concatenate([unsel_tail] * R_A, axis=0)], axis=1))

    def score_tile(kt, ms):
        c0 = pl.multiple_of(kt * KT, KT)
        out = []
        for g in range(G_A):
            s = _dot(lhs_mains[g], k_ext(g, c0, KT))
            sc_ref[g, :, pl.ds(c0, KT)] = s
            out.append(jnp.maximum(ms[g], jnp.max(s, axis=-1, keepdims=True)))
        return tuple(out)

    ms = lax.fori_loop(0, n_main, score_tile, tuple(jnp.full((R_A * TQ, 1), NEG, F32) for _ in range(G_A)))
    tails = []
    for g in range(G_A):
        d_here = dsel_ref[R_A * g:R_A * (g + 1)].reshape(R_A * TQ, 2 * TQ)
        s_a = _dot(lhs_tails[g], k_ext(g, c_a, TQ)) + d_here[:, 0:TQ] + jnp.where(i == 0, NEG, 0.0)
        s_b = _dot(lhs_tails[g], k_ext(g, c_b, TQ)) + d_here[:, TQ:2 * TQ]
        m = jnp.maximum(ms[g], jnp.maximum(jnp.max(s_a, axis=-1, keepdims=True),
                                           jnp.max(s_b, axis=-1, keepdims=True)))
        tails.append((s_a, s_b, m))

    def pv_tile(kt, accs):
        c0 = pl.multiple_of(kt * KT, KT)
        out = []
        for g in range(G_A):
            p = jnp.exp(sc_ref[g, :, pl.ds(c0, KT)] - tails[g][2]).astype(BF16)
            out.append(accs[g] + _dot_nt(p, v_ext(g, c0, KT)))
        return tuple(out)

    accs = lax.fori_loop(0, n_main, pv_tile,
                         tuple(jnp.zeros((R_A * TQ, HD_A + 16), F32) for _ in range(G_A)))

    for g in range(G_A):
        q2, o_cmp = q2s[g], o_cmps[g]
        s_a, s_b, m = tails[g]
        acc = (accs[g] + _dot_nt(jnp.exp(s_a - m).astype(BF16), v_ext(g, c_a, TQ))
               + _dot_nt(jnp.exp(s_b - m).astype(BF16), v_ext(g, c_b, TQ)))

        s_w = []
        for w in range(n_win_tiles):
            kt = i - (n_win_tiles - 1) + w
            c0 = pl.multiple_of(jnp.maximum(kt, 0) * TQ, TQ)
            kw = win_ref[0, HD_A * g:HD_A * (g + 1), pl.ds(c0, TQ)]
            bw = bwin_ref[R_A * g:R_A * (g + 1), :, w * TQ:(w + 1) * TQ].reshape(R_A * TQ, TQ)
            s_w.append(_dot(q2, kw) + bw + jnp.where(kt < 0, NEG, 0.0))
        s_w = jnp.concatenate(s_w, axis=1)
        p_w = jnp.exp(s_w - jnp.max(s_w, axis=-1, keepdims=True)).astype(BF16)
        acc_w = jnp.zeros((R_A * TQ, HD_A + 16), F32)
        for w in range(n_win_tiles):
            kt = i - (n_win_tiles - 1) + w
            c0 = pl.multiple_of(jnp.maximum(kt, 0) * TQ, TQ)
            vw = jnp.concatenate([win_ref[0, LANE + HD_A * g:LANE + HD_A * (g + 1), pl.ds(c0, TQ)],
                                  ones_rows[:, 0:TQ]], axis=0)
            acc_w = acc_w + _dot_nt(p_w[:, w * TQ:(w + 1) * TQ], vw)

        for r in range(R_A):
            h = R_A * g + r
            rows = slice(r * TQ, (r + 1) * TQ)
            c_sel = gates[:, 3 * h + 1:3 * h + 2] / acc[rows, HD_A:HD_A + 1]
            c_win = gates[:, 3 * h + 2:3 * h + 3] / acc_w[rows, HD_A:HD_A + 1]
            o_ref[:, HD_A * h:HD_A * (h + 1)] = (gates[:, 3 * h:3 * h + 1] * o_cmp[rows]
                                                  + c_sel * acc[rows, 0:HD_A] + c_win * acc_w[rows, 0:HD_A])


def _prompt_nsa(rel_bias, qa, kvb, cmpk, gates, movt, emat):
    nb, _, t = kvb.shape
    ni = t // TQ
    nch = cmpk.shape[1]
    n_win_tiles = WINDOW // TQ + 1

    def full(a):
        return pl.BlockSpec(a.shape, lambda i, b, _n=a.ndim: (0,) * _n)

    return pl.pallas_call(
        _prompt_nsa_kernel,
        grid=(ni, nb),
        in_specs=[pl.BlockSpec(memory_space=pltpu.SMEM),
                  pl.BlockSpec((TQ, W_A), lambda i, b: (b * ni + i, 0)),
                  pl.BlockSpec((1, 2 * LANE, t), lambda i, b: (b, 1, 0)),
                  pl.BlockSpec((1, 2 * LANE, t), lambda i, b: (b, 2, 0)),
                  pl.BlockSpec((1, nch, 2 * LANE), lambda i, b: (b, 0, 0)),
                  pl.BlockSpec((TQ, LANE), lambda i, b: (b * ni + i, 0)),
                  full(movt), full(emat)],
        out_specs=pl.BlockSpec((TQ, W_A), lambda i, b: (b * ni + i, 0)),
        out_shape=jax.ShapeDtypeStruct((nb * t, W_A), F32),
        scratch_shapes=[pltpu.VMEM((H_A, TQ, n_win_tiles * TQ), F32),
                        pltpu.VMEM((H_A, TQ, 2 * TQ), F32),
                        pltpu.VMEM((H_A, TQ, 2 * nch), F32),
                        pltpu.VMEM((H_A, TQ, nch), F32),
                        pltpu.VMEM((G_A, R_A * TQ, t), F32)],
        compiler_params=pltpu.CompilerParams(vmem_limit_bytes=VMEM_LIMIT,
                                             dimension_semantics=("arbitrary", "arbitrary")),
        name="prompt_nsa",
    )(rel_bias, qa, kvb, kvb, cmpk, gates, movt, emat)


def _prompt_mla_kernel(qm_ref, mla_ref, o_ref, sc_ref, acc_ref):
    i = pl.program_id(1)
    nrow = H_B * TQM
    n_full = (i * TQM) // KT
    q2 = jnp.concatenate([qm_ref[:, 2 * LANE * h:2 * LANE * (h + 1)] for h in range(H_B)], axis=0)

    def score_tile(kt, m):
        c0 = pl.multiple_of(kt * KT, KT)
        s = _dot(q2, mla_ref[0, :, pl.ds(c0, KT)])
        sc_ref[:, pl.ds(c0, KT)] = s
        return jnp.maximum(m, jnp.max(s, axis=-1, keepdims=True))

    m = lax.fori_loop(0, n_full, score_tile, jnp.full((nrow, 1), NEG, F32))
    c_d = pl.multiple_of(n_full * KT, KT)
    qpos = i * TQM + lax.broadcasted_iota(jnp.int32, (H_B, TQM, KT), 1).reshape(nrow, KT)
    kpos = c_d + lax.broadcasted_iota(jnp.int32, (nrow, KT), 1)
    s_d = jnp.where(kpos <= qpos, _dot(q2, mla_ref[0, :, pl.ds(c_d, KT)]), NEG)
    m = jnp.maximum(m, jnp.max(s_d, axis=-1, keepdims=True))
    acc_ref[...] = _dot_nt(jnp.exp(s_d - m).astype(BF16), mla_ref[0, :, pl.ds(c_d, KT)])

    def pv_tile(kt, carry):
        c0 = pl.multiple_of(kt * KT, KT)
        p = jnp.exp(sc_ref[:, pl.ds(c0, KT)] - m).astype(BF16)
        acc_ref[...] += _dot_nt(p, mla_ref[0, :, pl.ds(c0, KT)])
        return carry

    lax.fori_loop(0, n_full, pv_tile, 0)
    acc = acc_ref[...]
    o = acc[:, 0:KV_LORA] / acc[:, KV_LORA + ROPE_DIM:KV_LORA + ROPE_DIM + 1]
    for h in range(H_B):
        o_ref[:, KV_LORA * h:KV_LORA * (h + 1)] = o[h * TQM:(h + 1) * TQM]


def _prompt_mla(qm, mlab):
    nb, rows, t = mlab.shape
    ni = t // TQM
    return pl.pallas_call(
        _prompt_mla_kernel,
        grid=(nb, ni),
        in_specs=[pl.BlockSpec((TQM, H_B * 2 * LANE), lambda b, i: (b * ni + i, 0)),
                  pl.BlockSpec((1, rows, t), lambda b, i: (b, 0, 0))],
        out_specs=pl.BlockSpec((TQM, H_B * KV_LORA), lambda b, i: (b * ni + i, 0)),
        out_shape=jax.ShapeDtypeStruct((nb * t, H_B * KV_LORA), F32),
        scratch_shapes=[pltpu.VMEM((H_B * TQM, t), F32), pltpu.VMEM((H_B * TQM, rows), F32)],
        compiler_params=pltpu.CompilerParams(vmem_limit_bytes=VMEM_LIMIT),
        name="prompt_mla",
    )(qm, mlab)


def _merge_kernel(x_ref, gate_ref, oa_ref, ol_ref, sza_ref, szb_ref, wuv_ref, wout_ref, gf_ref, y_ref):
    ya = oa_ref[...] * sza_ref[...]
    ol = ol_ref[...]
    szb = szb_ref[...]
    yb = [_dot(ol[:, KV_LORA * h:KV_LORA * (h + 1)].astype(BF16), wuv_ref[h]) * szb[:, V_HEAD * h:V_HEAD * (h + 1)]
          for h in range(H_B)]
    yab = jnp.concatenate([ya] + yb, axis=1).astype(BF16)
    xn = x_ref[...] + gate_ref[0] * _dot(yab, wout_ref[...])
    y_ref[...] = xn * lax.rsqrt(jnp.mean(xn * xn, axis=-1, keepdims=True) + EPS) * gf_ref[...]


def _merge(xf, gate3, per_row_mod, tb, oa, ol, sza, szb, wuv, wout, gfin):
    n, d = xf.shape
    tm = _row_tile(tb)
    tiles_b = tb // tm
    if per_row_mod:
        mod_spec = pl.BlockSpec((1, tm, d), lambda r: (r, 0, 0))
    else:
        mod_spec = pl.BlockSpec((1, 1, d), lambda r: (r // tiles_b, 0, 0))

    def full(a):
        return pl.BlockSpec(a.shape, lambda r, _n=a.ndim: (0,) * _n)

    def tmaj(cols):
        return pl.BlockSpec((tm, cols), lambda r: (r, 0))

    return pl.pallas_call(
        _merge_kernel,
        grid=(n // tm,),
        in_specs=[tmaj(d), mod_spec, tmaj(W_A), tmaj(W_B), tmaj(W_A), tmaj(W_B), full(wuv), full(wout), full(gfin)],
        out_specs=tmaj(d),
        out_shape=jax.ShapeDtypeStruct((n, d), F32),
        compiler_params=pltpu.CompilerParams(vmem_limit_bytes=VMEM_LIMIT),
        name="out_merge",
    )(xf, gate3, oa, ol, sza, szb, wuv, wout, gfin)


def _row_head_bias(bucket, tab_ref, g, shift):
    ts = bucket.shape[0] // R_A
    row_r = lax.broadcasted_iota(jnp.int32, bucket.shape, 0) // ts
    out = jnp.zeros(bucket.shape, F32)
    for r in range(R_A):
        h = R_A * g + r
        val = _bias_lookup(bucket, tab_ref, h)
        if shift:
            val = val - tab_ref[N_BUCKETS - 1, h]
        out = jnp.where(row_r == r, val, out)
    return out


def _gather_pages(pt_ref, cache_hbm, buf_ref, sem_ref, n_pages):
    b = pl.program_id(0)
    nseq = pl.num_programs(0)
    nslot = buf_ref.shape[0]
    ahead = nslot - 1

    def copy(seq, k, slot):
        return pltpu.make_async_copy(cache_hbm.at[pt_ref[seq, k]], buf_ref.at[slot, k], sem_ref.at[slot])

    def start(seq, slot):
        for k in range(n_pages):
            copy(seq, k, slot).start()

    for s0 in range(ahead):
        pl.when((b == 0) & (s0 < nseq))(functools.partial(start, s0, s0))
    pl.when(b + ahead < nseq)(functools.partial(start, b + ahead, (b + ahead) % nslot))

    slot = b % nslot
    for k in range(n_pages):
        copy(b, k, slot).wait()
    return slot


def _own_lanes(x, b, ts):
    per_block = LANE // ts
    shift = (LANE - ts * (b % per_block)) % LANE
    lane = lax.broadcasted_iota(jnp.int32, x.shape, 1)
    return jnp.where(lane < ts, pltpu.roll(x, shift, 1), 0.0)


def _decode_nsa_kernel(pt_ref, tab_ref, cache_hbm, qa_ref, nsel_ref, nwin_ref, win_ref, gsel_ref, lperm_ref,
                       wc_ref, pe_ref, wflat_ref, mov_ref, o_ref, wout_ref, a_ref, stash_ref, bc_ref, ds_ref,
                       dn_ref, bw_ref, bwn_ref, buf_ref, sem_ref, *, ts, past):
    b = pl.program_id(0)
    n_pages = buf_ref.shape[1]
    slot = _gather_pages(pt_ref, cache_hbm, buf_ref, sem_ref, n_pages)
    nch = past // CMP_STRIDE
    nc = nch - 1
    nrow = R_A * ts
    n_cmp_rows = 2 * G_A * HD_A
    wb = win_ref.shape[2]

    for pair in range(n_pages // 2):
        pa = buf_ref[slot, 2 * pair]
        pb = buf_ref[slot, 2 * pair + 1]
        both = jnp.concatenate([pa[0:n_cmp_rows], pb[0:n_cmp_rows]], axis=1).astype(BF16)
        tb = _chunk_rows(lperm_ref, both)
        for j in range(CMP_STRIDE):
            a_ref[j, pair * 16:(pair + 1) * 16, :] = tb[16 * j:16 * (j + 1)]
        stash_ref[:, pair * 256:(pair + 1) * 256] = jnp.concatenate(
            [pa[n_cmp_rows:2 * n_cmp_rows], pb[n_cmp_rows:2 * n_cmp_rows]], axis=1).astype(BF16)

    trow = lax.broadcasted_iota(jnp.int32, (nrow, 1), 0) % ts

    @pl.when(b == 0)
    def _():
        n = lax.broadcasted_iota(jnp.int32, (nrow, nch), 1)
        dist = past + trow - (n * CMP_STRIDE + CMP_BLOCK - 1)
        u = lax.broadcasted_iota(jnp.int32, (nrow, LANE), 1)
        dist_s = LANE + trow - u
        dist_n = trow - u
        kl = lax.broadcasted_iota(jnp.int32, (nrow, wb), 1)
        dist_w = wb + trow - kl
        ok_n = (dist_n >= 0) & (u < ts)
        for g in range(G_A):
            bc_ref[g] = jnp.where((dist >= 0) & (n < nc), _row_head_bias(_rel_bucket(dist), tab_ref, g, False), NEG)
            ds_ref[g] = _row_head_bias(_rel_bucket(dist_s), tab_ref, g, True)
            dn_ref[g] = jnp.where(ok_n, _row_head_bias(_rel_bucket(dist_n), tab_ref, g, True), NEG)
            bw_ref[g] = jnp.where(dist_w < WINDOW, _row_head_bias(_rel_bucket(dist_w), tab_ref, g, False), NEG)
            bwn_ref[g] = jnp.where(ok_n, _row_head_bias(_rel_bucket(dist_n), tab_ref, g, False), NEG)

    cmpk = _compress_chunks(a_ref, wc_ref, pe_ref, wflat_ref, nch).astype(BF16)
    new_sel = _own_lanes(nsel_ref[0], b, ts)
    new_win = _own_lanes(nwin_ref[0], b, ts)
    nsp = mov_ref.shape[1]
    nbp = past // SEL_BLOCK
    jj = lax.broadcasted_iota(jnp.int32, (nrow, nsp), 1)
    cc = (past + trow) // SEL_BLOCK
    o_cmps, tots = [], []
    for g in range(G_A):
        q2 = qa_ref[0, g]
        s_c = _dot_nt(q2, cmpk[:, HD_A * g:HD_A * (g + 1)]) + bc_ref[g]
        p_c = _softmax_rows(s_c).astype(BF16)
        o_cmps.append(_dot(p_c, cmpk[:, LANE + HD_A * g:LANE + HD_A * (g + 1)]))
        imp = _dot(p_c, mov_ref[...])
        tot = imp
        for r in range(1, R_A):
            tot = tot + pltpu.roll(imp, r * ts, 0)
        tots.append(tot)
    half = nrow // 2
    first = lax.broadcasted_iota(jnp.int32, (half, nsp), 0) < ts
    jj8, cc8 = jj[0:half], cc[0:half]
    forced = (jj8 == 0) | (jj8 == cc8) | (jj8 == cc8 - 1)
    mixed = jnp.where(first, tots[0][0:half], tots[1][0:half])
    score = jnp.where(forced, FORCE_SCORE, jnp.where(jj8 <= cc8, mixed, -jnp.inf))
    rank = _rank(score, nbp + 1, 1)
    rank_sw = pltpu.roll(rank, ts, 0)
    allowed = jj8 <= cc8
    sel8 = [((jnp.where(first, rank, rank_sw) < N_SEL) & allowed).astype(F32)[:, 0:nbp],
            ((jnp.where(first, rank_sw, rank) < N_SEL) & allowed).astype(F32)[:, 0:nbp]]
    low = lax.broadcasted_iota(jnp.int32, (half, LANE), 1) < SEL_BLOCK
    madds = []
    for g in range(G_A):
        flags = [jnp.where(low, sel8[g][:, 2 * c:2 * c + 1], sel8[g][:, 2 * c + 1:2 * c + 2])
                 for c in range(nbp // 2)]
        m8 = (jnp.concatenate(flags, axis=1) - 1.0) * (-NEG)
        madds.append(jnp.concatenate([m8, m8], axis=0))
    for g in range(G_A):
        q2 = qa_ref[0, g]
        o_cmp = o_cmps[g]
        s_p = _dot(q2, stash_ref[HD_A * g:HD_A * (g + 1), :]) + madds[g]
        s_last = s_p[:, past - LANE:past] + ds_ref[g]
        k_new = new_sel[HD_A * g:HD_A * (g + 1)].astype(BF16)
        v_new = new_sel[LANE + HD_A * g:LANE + HD_A * (g + 1)].astype(BF16)
        s_n = _dot(q2, k_new) + dn_ref[g]
        s_all = jnp.concatenate([s_p[:, 0:past - LANE], s_last, s_n], axis=1)
        m = jnp.max(s_all, axis=-1, keepdims=True)
        p = jnp.exp(s_all - m)
        l = jnp.sum(p, axis=-1, keepdims=True)
        pb16 = p.astype(BF16)
        o_sel = (_dot_nt(pb16[:, 0:past], stash_ref[LANE + HD_A * g:LANE + HD_A * (g + 1), :])
                 + _dot_nt(pb16[:, past:past + LANE], v_new)) / l
        kw = win_ref[0, HD_A * g:HD_A * (g + 1), :].astype(BF16)
        vw = win_ref[0, LANE + HD_A * g:LANE + HD_A * (g + 1), :].astype(BF16)
        kw_new = new_win[HD_A * g:HD_A * (g + 1)].astype(BF16)
        vw_new = new_win[LANE + HD_A * g:LANE + HD_A * (g + 1)].astype(BF16)
        s_w = jnp.concatenate([_dot(q2, kw) + bw_ref[g], _dot(q2, kw_new) + bwn_ref[g]], axis=1)
        p_w = _softmax_rows(s_w).astype(BF16)
        o_win = _dot_nt(p_w[:, 0:wb], vw) + _dot_nt(p_w[:, wb:wb + LANE], vw_new)
        gs = gsel_ref[0, g]
        o_ref[0, g] = gs[:, 0:1] * o_cmp + gs[:, 1:2] * o_sel + gs[:, 2:3] * o_win

    win = win_ref[0]
    rolled = pltpu.roll(win, wb - ts, 1)
    new_w = pltpu.roll(new_win, LANE - ts, 1)
    lane = lax.broadcasted_iota(jnp.int32, (win.shape[0], LANE), 1)
    last = jnp.where(lane >= LANE - ts, new_w, rolled[:, wb - LANE:wb])
    wout_ref[0] = jnp.concatenate([rolled[:, 0:wb - LANE], last], axis=1)


def _decode_nsa(page_table, rel_bias, cache_t, qa_s, nsat_s, wint_s, win_t, gsel, cw, mov, ts, past):
    bd, n_pages = page_table.shape
    lperm, wc, pe, wflat = cw
    nrow = R_A * ts
    nch = past // CMP_STRIDE
    wb = win_t.shape[2]
    feat = cache_t.shape[1]
    per_block = LANE // ts

    def full(a):
        return pl.BlockSpec(a.shape, lambda b, pt, _n=a.ndim: (0,) * _n, pipeline_mode=pl.Buffered(1))

    in_specs = [pl.BlockSpec(memory_space=pltpu.SMEM),
                pl.BlockSpec(memory_space=pl.ANY),
                pl.BlockSpec((1, G_A, nrow, HD_A), lambda b, pt: (b, 0, 0, 0)),
                pl.BlockSpec((1, 2 * LANE, LANE), lambda b, pt: (0, 1, b // per_block)),
                pl.BlockSpec((1, 2 * LANE, LANE), lambda b, pt: (0, 0, b // per_block)),
                pl.BlockSpec((1, 2 * LANE, wb), lambda b, pt: (b, 0, 0)),
                pl.BlockSpec((1, G_A, nrow, LANE), lambda b, pt: (b, 0, 0, 0)),
                full(lperm), full(wc), full(pe), full(wflat), full(mov)]
    grid_spec = pltpu.PrefetchScalarGridSpec(
        num_scalar_prefetch=1,
        grid=(bd,),
        in_specs=in_specs,
        out_specs=[pl.BlockSpec((1, G_A, nrow, HD_A), lambda b, pt: (b, 0, 0, 0)),
                   pl.BlockSpec((1, 2 * LANE, wb), lambda b, pt: (b, 0, 0))],
        scratch_shapes=[pltpu.VMEM((CMP_STRIDE, nch, 2 * LANE), BF16),
                        pltpu.VMEM((2 * LANE, past), BF16),
                        pltpu.VMEM((G_A, nrow, nch), F32),
                        pltpu.VMEM((G_A, nrow, LANE), F32),
                        pltpu.VMEM((G_A, nrow, LANE), F32),
                        pltpu.VMEM((G_A, nrow, wb), F32),
                        pltpu.VMEM((G_A, nrow, LANE), F32),
                        pltpu.VMEM((2, n_pages, feat, LANE), F32),
                        pltpu.SemaphoreType.DMA((2,))])
    return pl.pallas_call(
        functools.partial(_decode_nsa_kernel, ts=ts, past=past),
        grid_spec=grid_spec,
        out_shape=(jax.ShapeDtypeStruct((bd, G_A, nrow, HD_A), F32),
                   jax.ShapeDtypeStruct((bd, 2 * LANE, wb), F32)),
        compiler_params=pltpu.CompilerParams(vmem_limit_bytes=VMEM_LIMIT,
                                             dimension_semantics=("arbitrary",)),
        name="decode_nsa",
    )(page_table, rel_bias, cache_t, qa_s, nsat_s, wint_s, win_t, gsel, lperm, wc, pe, wflat, mov)


def _decode_mla_kernel(pt_ref, cache_hbm, qm_ref, new_ref, o_ref, buf_ref, sem_ref, *, ts, n_pages):
    slot = _gather_pages(pt_ref, cache_hbm, buf_ref, sem_ref, n_pages)
    nrow = H_B * ts
    q2 = qm_ref[0]
    q_lat, q_rope = q2[:, 0:KV_LORA], q2[:, KV_LORA:KV_LORA + ROPE_DIM]
    feat = KV_LORA + ROPE_DIM

    def scores_of(kt):
        return _dot(q_lat, kt[0:KV_LORA]) + _dot(q_rope, kt[KV_LORA:feat])

    keys, scores = [], []
    for grp in range(n_pages // MLA_PAGE_GROUP):
        kt = jnp.concatenate([buf_ref[slot, grp * MLA_PAGE_GROUP + k].astype(BF16)
                              for k in range(MLA_PAGE_GROUP)], axis=1)
        keys.append(kt)
        scores.append(scores_of(kt))
    kn = _own_lanes(new_ref[0], pl.program_id(0), ts).astype(BF16)
    trow = lax.broadcasted_iota(jnp.int32, (nrow, LANE), 0) % ts
    u = lax.broadcasted_iota(jnp.int32, (nrow, LANE), 1)
    scores.append(scores_of(kn) + jnp.where((u <= trow) & (u < ts), 0.0, NEG))
    s_all = jnp.concatenate(scores, axis=1)
    m = jnp.max(s_all, axis=-1, keepdims=True)
    p = jnp.exp(s_all - m)
    l = jnp.sum(p, axis=-1, keepdims=True)
    pb = p.astype(BF16)
    gw = MLA_PAGE_GROUP * LANE
    acc = _dot_nt(pb[:, len(keys) * gw:len(keys) * gw + LANE], kn[0:KV_LORA])
    for grp, kt in enumerate(keys):
        acc = acc + _dot_nt(pb[:, grp * gw:(grp + 1) * gw], kt[0:KV_LORA])
    o_ref[0] = acc / l


def _decode_mla(page_table, cache_t, qm_s, new_mla, ts):
    bd, n_pages = page_table.shape
    nrow = H_B * ts
    feat = cache_t.shape[1]

    grid_spec = pltpu.PrefetchScalarGridSpec(
        num_scalar_prefetch=1,
        grid=(bd,),
        in_specs=[pl.BlockSpec(memory_space=pl.ANY),
                  pl.BlockSpec((1, nrow, 2 * LANE), lambda b, pt: (b, 0, 0)),
                  pl.BlockSpec((1, feat, LANE), lambda b, pt: (0, 0, b // (LANE // ts)))],
        out_specs=pl.BlockSpec((1, nrow, KV_LORA), lambda b, pt: (b, 0, 0)),
        scratch_shapes=[pltpu.VMEM((MLA_SLOTS, n_pages, feat, LANE), F32),
                        pltpu.SemaphoreType.DMA((MLA_SLOTS,))])
    return pl.pallas_call(
        functools.partial(_decode_mla_kernel, ts=ts, n_pages=n_pages),
        grid_spec=grid_spec,
        out_shape=jax.ShapeDtypeStruct((bd, nrow, KV_LORA), F32),
        compiler_params=pltpu.CompilerParams(vmem_limit_bytes=VMEM_LIMIT,
                                             dimension_semantics=("arbitrary",)),
        name="decode_mla",
    )(page_table, cache_t, qm_s, new_mla)


def _rope_tables(pos):
    half = ROPE_DIM // 2
    inv = ROPE_BASE ** (-jnp.arange(half, dtype=F32) / half)
    ang = pos.astype(F32)[:, None] * inv[None, :]
    cos, sin = jnp.cos(ang), jnp.sin(ang)
    z = jnp.zeros((pos.shape[0], LANE - ROPE_DIM), F32)
    cos_tm = jnp.concatenate([cos, cos, z], axis=1)
    sin_tm = jnp.concatenate([sin, sin, z], axis=1)
    return cos_tm, sin_tm, cos.T, sin.T


def _overlap_t(n_cmp_rows, n_sel):
    cs = jnp.arange(n_cmp_rows)[None, :] * CMP_STRIDE
    ss = jnp.arange(n_sel)[:, None] * SEL_BLOCK
    ov = jnp.minimum(cs + CMP_BLOCK, ss + SEL_BLOCK) - jnp.maximum(cs, ss)
    ov = jnp.maximum(ov, 0).astype(F32) / CMP_BLOCK
    return jnp.where(jnp.arange(n_cmp_rows)[None, :] < n_cmp_rows - 1, ov, 0.0)


def _compress_weights(w_k, w_v, pe_k, pe_v):
    eye = jnp.eye(G_A, dtype=F32)

    def pair_blocks(w):
        w6 = w.reshape(2, CMP_STRIDE // 2, 2, HD_A, HD_A)
        blk = jnp.einsum('hpjde,gk->pjgdhke', w6, eye)
        return blk.reshape(CMP_STRIDE // 2, 2 * LANE, 2 * LANE)

    wc = jnp.stack([pair_blocks(w_k), pair_blocks(w_v)]).astype(BF16)
    pe = jnp.stack([jnp.tile(pe_k.reshape(1, -1), (8, 1)), jnp.tile(pe_v.reshape(1, -1), (8, 1))]).astype(BF16)
    wflat = jnp.stack([jnp.tile(w_k.reshape(-1, HD_A), (1, G_A)),
                       jnp.tile(w_v.reshape(-1, HD_A), (1, G_A))]).astype(BF16)
    r = jnp.arange(256)
    lperm = (jnp.arange(256)[None, :] == ((r % 16) * 16 + r // 16)[:, None]).astype(BF16)
    return lperm, wc, pe, wflat


def kernel(x_prompt, x_sample, cache_nsa, cache_mla, state_win, page_table, c_prompt, c_sample,
           w_ada, b_ada, g_norm, w_in, w_cmp_k, w_cmp_v, pe_cmp_k, pe_cmp_v, rel_bias,
           g_q_lat, w_uq, g_kv_lat, w_uk, w_uv, w_out, g_final):
    assert w_in.shape[0] == 1, "one layer"
    nb, t, d = x_prompt.shape
    bd, ts, _ = x_sample.shape
    n_pages = page_table.shape[1]
    page = cache_nsa.shape[2]
    past = n_pages * page
    assert page == LANE and (past + ts) // CMP_STRIDE == past // CMP_STRIDE and ts <= SEL_BLOCK
    assert (R_A * ts // 2) % ts == 0 and t % TQM == 0 and LANE % ts == 0 and (bd * ts) % LANE == 0
    assert t % KT == 0 and (bd * ts) % (TM // 2) == 0 and n_pages % MLA_PAGE_GROUP == 0 and n_pages % 2 == 0
    wb = state_win.shape[2]

    wi = w_in[0]
    o1 = W_A
    o2 = o1 + 6 * G_A * HD_A
    o3 = o2 + 3 * H_A
    o4 = o3 + W_A
    o5 = o4 + Q_LORA
    o6 = o5 + KV_LORA
    o7 = o6 + ROPE_DIM
    wg = jnp.pad(wi[:, o2:o3], ((0, 0), (0, LANE - 3 * H_A)))
    wt = jnp.concatenate([wi[:, 0:o1] * (HD_A ** -0.5), wi[:, o3:o4], wi[:, o7:], wi[:, o4:o5], wg], axis=1).astype(BF16)
    wf = jnp.concatenate([wi[:, o1:o2], wi[:, o5:o6], wi[:, o6:o7]], axis=1).T.astype(BF16)
    wq = w_uq[0]
    wqn = wq[:, :, :NOPE].reshape(Q_LORA, H_B * NOPE).astype(BF16)
    wr = wq[:, :, NOPE:]
    half = ROPE_DIM // 2
    wsw = jnp.concatenate([-wr[..., half:], wr[..., :half]], axis=-1)
    padr = ((0, 0), (0, 0), (0, LANE - ROPE_DIM))
    wqr = jnp.pad(wr, padr).reshape(Q_LORA, H_B * LANE).astype(BF16)
    wqs = jnp.pad(wsw, padr).reshape(Q_LORA, H_B * LANE).astype(BF16)
    wuk = jnp.transpose(w_uk[0], (1, 2, 0)).astype(BF16)
    wuv = jnp.transpose(w_uv[0], (1, 0, 2)).astype(BF16)
    wts = (g_norm[0][None], wt, wf, g_q_lat[0][None], wqn, wqr, wqs, wuk, g_kv_lat[0][:, None])
    cw = _compress_weights(w_cmp_k[0], w_cmp_v[0], pe_cmp_k[0], pe_cmp_v[0])
    wout = w_out[0].astype(BF16)
    gfin = g_final[None]

    n_c = nb + bd
    n_cp = -(-n_c // 8) * 8
    c_all = jnp.pad(jnp.concatenate([c_prompt, c_sample], axis=0), ((0, n_cp - n_c), (0, 0)))
    mod = _modulation(c_all, w_ada[0], b_ada)
    shift, scale, gate = mod[:, 0:d], mod[:, d:2 * d], mod[:, 2 * d:3 * d]

    def prompt_mod(a):
        return a[0:nb][:, None, :]

    def sample_mod(a):
        tm_s = _row_tile(bd * ts)
        return jnp.repeat(a[nb:nb + bd], ts, axis=0).reshape(bd * ts // tm_s, tm_s, d)

    pos_p = jnp.arange(t, dtype=jnp.int32)
    xp = x_prompt.reshape(nb * t, d)
    (qa, gates, sza, szb, qm, nsat, wint, kvb, mlat, mlab) = _project(
        xp, prompt_mod(shift), prompt_mod(scale), False, nb, t, wts, *_rope_tables(pos_p))
    cmpk = _prompt_compress(kvb, cw)
    nch_p = t // CMP_STRIDE
    ns_p = t // SEL_BLOCK
    movt = _overlap_t(nch_p, ns_p).astype(BF16)
    emat = jnp.where(jnp.arange(t)[None, :] // SEL_BLOCK == jnp.arange(ns_p)[:, None], NEG, 0.0).astype(BF16)
    oa = _prompt_nsa(rel_bias, qa, kvb, cmpk, gates, movt, emat)
    ol = _prompt_mla(qm, mlab)
    y_prompt = _merge(xp, prompt_mod(gate), False, t, oa, ol, sza, szb, wuv, wout, gfin).reshape(nb, t, d)

    n_s = bd * ts
    pos_s = past + jnp.arange(ts, dtype=jnp.int32)
    cos_tm, sin_tm, cos_t, sin_t = _rope_tables(pos_s)
    xs = x_sample.reshape(n_s, d)
    (qa_s, gates_s, sza_s, szb_s, qm_s, nsat_s, wint_s, _, mlat_s, _) = _project(
        xs, sample_mod(shift), sample_mod(scale), True, 1, n_s, wts,
        jnp.tile(cos_tm, (bd, 1)), jnp.tile(sin_tm, (bd, 1)), jnp.tile(cos_t, (1, bd)), jnp.tile(sin_t, (1, bd)))
    nrow = R_A * ts
    qa_g = qa_s.reshape(bd, ts, G_A, R_A, HD_A).transpose(0, 2, 3, 1, 4).reshape(bd, G_A, nrow, HD_A)
    gsel = gates_s[:, 0:3 * H_A].reshape(bd, ts, G_A, R_A, 3).transpose(0, 2, 3, 1, 4).reshape(bd, G_A, nrow, 3)
    gsel = jnp.pad(gsel, ((0, 0), (0, 0), (0, 0), (0, LANE - 3)))
    qm_h = qm_s.reshape(bd, ts, H_B, 2 * LANE).transpose(0, 2, 1, 3).reshape(bd, H_B * ts, 2 * LANE)
    n_phys = cache_nsa.shape[1]
    cache_nsa_t = jnp.transpose(cache_nsa[0], (0, 2, 3, 4, 1)).reshape(n_phys, 4 * G_A * HD_A, page)
    cache_mla_t = jnp.transpose(cache_mla[0], (0, 2, 1))
    win_t = jnp.transpose(state_win[0], (0, 2, 3, 4, 1)).reshape(bd, 2 * G_A * HD_A, wb)
    nch_s = past // CMP_STRIDE
    nbp = past // SEL_BLOCK
    nsp = -(-(nbp + 1) // LANE) * LANE
    mov_s = jnp.pad(_overlap_t(nch_s, nbp + 1).T, ((0, 0), (0, nsp - nbp - 1))).astype(BF16)
    oa_g, win_new = _decode_nsa(page_table, rel_bias, cache_nsa_t, qa_g, nsat_s, wint_s, win_t, gsel, cw,
                                mov_s, ts, past)
    ol_h = _decode_mla(page_table, cache_mla_t, qm_h, mlat_s, ts)
    oa_s = oa_g.reshape(bd, G_A, R_A, ts, HD_A).transpose(0, 3, 1, 2, 4).reshape(n_s, W_A)
    ol_s = ol_h.reshape(bd, H_B, ts, KV_LORA).transpose(0, 2, 1, 3).reshape(n_s, H_B * KV_LORA)
    y_sample = _merge(xs, sample_mod(gate), True, n_s, oa_s, ol_s, sza_s, szb_s, wuv, wout, gfin).reshape(bd, ts, d)

    new_nsa_prompt = nsat.reshape(nb, 4, G_A, HD_A, t).transpose(0, 4, 1, 2, 3)[None]
    new_nsa_sample = nsat_s[0].reshape(4, G_A, HD_A, bd, ts).transpose(3, 4, 0, 1, 2)[None]
    new_mla_prompt = mlat.transpose(0, 2, 1)[None]
    new_mla_sample = mlat_s[0].reshape(KV_LORA + ROPE_DIM, bd, ts).transpose(1, 2, 0)[None]
    wlen = min(WINDOW, t)
    new_win_prompt = wint[:, :, t - wlen:].reshape(nb, 2, G_A, HD_A, wlen).transpose(0, 4, 1, 2, 3)[None]
    new_win_sample = win_new.reshape(bd, 2, G_A, HD_A, wb).transpose(0, 4, 1, 2, 3)[None]
    return (y_prompt, y_sample, new_nsa_prompt, new_nsa_sample, new_mla_prompt, new_mla_sample,
            new_win_prompt, new_win_sample)
```
